```python
import functools
import jax, jax.numpy as jnp
from jax import lax
import numpy as np

D_MODEL = 1024
BATCH = 4
SEQ = 8192
DEPTH = 2
DEC_BATCH = 32
DEC_SEQ = 8
PAST_LEN = 16384
PAGE_SIZE = 128

N_HEADS = 8
HEAD_DIM = 64
ATT_W = N_HEADS * HEAD_DIM
LRU_W = 512
LRU_BLOCKS = 8
LRU_BLOCK_W = LRU_W // LRU_BLOCKS
LRU_CONV = 4
LRU_C = 8.0
SC_W = 512
SC_CONV = 3
N_BRANCH = 3
D_FF = -(-8 * D_MODEL // (3 * 256)) * 256
Q_BLOCK = 128
RMS_EPS = 1e-6
FORGET_BIAS = 7.0
IN_COLS = 3 * ATT_W + N_HEADS + LRU_W + 3 * SC_W
NEG_INF = -1e30

kernel_name = 'hybrid_fox_rglru_shortconv_decode_step'


def rmsnorm(x, g):
    xf = x.astype(jnp.float32)
    ms = jnp.mean(xf * xf, axis=-1, keepdims=True)
    return (xf * lax.rsqrt(ms + RMS_EPS) * g.astype(jnp.float32)).astype(x.dtype)


def causal_dwconv(u, buf, w):
    width = w.shape[0]
    t_len = u.shape[1]
    up = jnp.concatenate([buf.astype(u.dtype), u], axis=1)
    y = up[:, 0:t_len] * w[0]
    for j in range(1, width):
        y = y + up[:, j:j + t_len] * w[j]
    return y, up[:, t_len:]


def rglru(xc, h0, w_a, b_a, w_x, b_x, lam):
    b, t_len = xc.shape[0], xc.shape[1]
    xb = xc.reshape(b, t_len, LRU_BLOCKS, LRU_BLOCK_W)
    gr = jnp.einsum('btnd,nde->btne', xb, w_a).reshape(b, t_len, LRU_W) + b_a
    gi = jnp.einsum('btnd,nde->btne', xb, w_x).reshape(b, t_len, LRU_W) + b_x
    r = jax.nn.sigmoid(gr.astype(jnp.float32))
    i = jax.nn.sigmoid(gi.astype(jnp.float32))
    log_a = -LRU_C * r * jax.nn.softplus(-lam.astype(jnp.float32))
    a = jnp.exp(log_a)
    mult = jnp.sqrt(-jnp.expm1(2.0 * log_a))
    bx = mult * i * xc.astype(jnp.float32)
    bx = bx.at[:, 0].add(a[:, 0] * h0.astype(jnp.float32))

    def combine(left, right):
        a1, b1 = left
        a2, b2 = right
        return a1 * a2, a2 * b1 + b2

    _, h = lax.associative_scan(combine, (a, bx), axis=1)
    return h.astype(xc.dtype), h[:, -1]


def fox_prompt(q, k, v, logf):
    b, s_len = q.shape[0], q.shape[1]
    scale = HEAD_DIM ** -0.5
    f_cum = jnp.cumsum(logf, axis=1).transpose(0, 2, 1)
    k_pos = jnp.arange(s_len)

    def block(i):
        start = i * Q_BLOCK
        qb = lax.dynamic_slice_in_dim(q, start, Q_BLOCK, axis=1)
        fq = lax.dynamic_slice_in_dim(f_cum, start, Q_BLOCK, axis=2)
        s = (jnp.einsum('bqhd,bkhd->bhqk', qb, k).astype(jnp.float32) * scale
             + (fq[..., None] - f_cum[:, :, None, :]))
        q_pos = start + jnp.arange(Q_BLOCK)
        s = jnp.where(q_pos[:, None] >= k_pos[None, :], s, NEG_INF)
        p = jax.nn.softmax(s, axis=-1)
        return jnp.einsum('bhqk,bkhd->bqhd', p.astype(v.dtype), v)

    out = lax.map(block, jnp.arange(s_len // Q_BLOCK))
    return out.transpose(1, 0, 2, 3, 4).reshape(b, s_len, ATT_W)


def fox_sample(k_past, v_past, logf_past, q, k, v, logf):
    b, t_len = q.shape[0], q.shape[1]
    p_len = k_past.shape[1]
    scale = HEAD_DIM ** -0.5
    f_cum = jnp.cumsum(jnp.concatenate([logf_past.astype(jnp.float32), logf], axis=1),
                       axis=1).transpose(0, 2, 1)
    fq = f_cum[:, :, p_len:]
    s_past = (jnp.einsum('bthd,bshd->bhts', q, k_past).astype(jnp.float32) * scale
              + (fq[..., None] - f_cum[:, :, None, :p_len]))
    s_new = (jnp.einsum('bthd,bshd->bhts', q, k).astype(jnp.float32) * scale
             + (fq[..., None] - fq[:, :, None, :]))
    causal = jnp.arange(t_len)[:, None] >= jnp.arange(t_len)[None, :]
    s_new = jnp.where(causal, s_new, NEG_INF)
    p = jax.nn.softmax(jnp.concatenate([s_past, s_new], axis=-1), axis=-1)
    out = (jnp.einsum('bhts,bshd->bthd', p[..., :p_len].astype(v.dtype), v_past)
           + jnp.einsum('bhts,bshd->bthd', p[..., p_len:].astype(v.dtype), v))
    return out.reshape(b, t_len, ATT_W)


def trunk_layer(x, attend, lru_buf, lru_h, sc_buf, norm1_g, w_in, b_f, lru_conv_w, lru_conv_b,
                lru_w_a, lru_b_a, lru_w_x, lru_b_x, lru_lambda, sc_conv_w, w_gate, b_gate,
                w_br_att, w_br_lru, w_br_sc, w_out, norm2_g, w_ffn_in, w_ffn_out):
    b, t_len, _ = x.shape
    xn = rmsnorm(x, norm1_g)
    z = xn @ w_in
    sizes = [ATT_W, ATT_W, ATT_W, N_HEADS, LRU_W, SC_W, SC_W, SC_W]
    idx = [int(c) for c in np.cumsum(sizes)[:-1]]
    q, k, v, fl, xr, bg, cg, hs = jnp.split(z, idx, axis=-1)
    q = q.reshape(b, t_len, N_HEADS, HEAD_DIM)
    k = k.reshape(b, t_len, N_HEADS, HEAD_DIM)
    v = v.reshape(b, t_len, N_HEADS, HEAD_DIM)
    logf = jax.nn.log_sigmoid(fl.astype(jnp.float32) + b_f.astype(jnp.float32))
    att = attend(q, k, v, logf)
    xc, new_lru_buf = causal_dwconv(xr, lru_buf, lru_conv_w)
    xc = xc + lru_conv_b
    lru, h_last = rglru(xc, lru_h, lru_w_a, lru_b_a, lru_w_x, lru_b_x, lru_lambda)
    uc, new_sc_buf = causal_dwconv(cg * hs, sc_buf, sc_conv_w)
    sc = bg * uc
    gates = jax.nn.sigmoid((xn @ w_gate + b_gate).astype(jnp.float32)).astype(x.dtype)
    gates = gates.reshape(b, t_len, N_BRANCH, D_MODEL)
    merged = (gates[:, :, 0] * (att @ w_br_att) + gates[:, :, 1] * (lru @ w_br_lru)
              + gates[:, :, 2] * (sc @ w_br_sc))
    h = x + merged @ w_out
    g, u = jnp.split(rmsnorm(h, norm2_g) @ w_ffn_in, 2, axis=-1)
    y = h + (jax.nn.silu(g) * u) @ w_ffn_out
    return y, (k, v, logf, h_last, new_lru_buf, new_sc_buf)


def setup_inputs(seed: int = 0) -> dict:
    key = jax.random.key(seed)
    ks = iter(jax.random.split(key, 40))
    f32 = jnp.float32
    n_pages = PAST_LEN // PAGE_SIZE
    n_pool = (5 * DEC_BATCH * n_pages) // 4

    def nrm(shape, scale):
        return jax.random.normal(next(ks), shape, f32) * scale

    x_prompt = nrm((BATCH, SEQ, D_MODEL), 1.0)
    x_sample = nrm((DEC_BATCH, DEC_SEQ, D_MODEL), 1.0)
    cache_k = nrm((DEPTH, n_pool, PAGE_SIZE, N_HEADS, HEAD_DIM), 1.0)
    cache_v = nrm((DEPTH, n_pool, PAGE_SIZE, N_HEADS, HEAD_DIM), 1.0)
    cache_logf = jax.nn.log_sigmoid(FORGET_BIAS + nrm((DEPTH, n_pool, PAGE_SIZE, N_HEADS), 1.0))
    state_lru_h = nrm((DEPTH, DEC_BATCH, LRU_W), 0.5)
    state_lru_conv = nrm((DEPTH, DEC_BATCH, LRU_CONV - 1, LRU_W), 1.0)
    state_sc_conv = nrm((DEPTH, DEC_BATCH, SC_CONV - 1, SC_W), 1.0)
    perm = jax.random.permutation(next(ks), n_pool)[:DEC_BATCH * n_pages]
    page_table = perm.reshape(DEC_BATCH, n_pages).astype(jnp.int32)

    u_lam = jax.random.uniform(next(ks), (DEPTH, LRU_W), f32, 0.9, 0.999)
    return {
        'x_prompt': x_prompt, 'x_sample': x_sample,
        'cache_k': cache_k, 'cache_v': cache_v, 'cache_logf': cache_logf,
        'state_lru_h': state_lru_h, 'state_lru_conv': state_lru_conv, 'state_sc_conv': state_sc_conv,
        'page_table': page_table,
        'norm1_g': 1.0 + nrm((DEPTH, D_MODEL), 0.05),
        'w_in': nrm((DEPTH, D_MODEL, IN_COLS), D_MODEL ** -0.5),
        'b_f': FORGET_BIAS + nrm((DEPTH, N_HEADS), 0.5),
        'lru_conv_w': nrm((DEPTH, LRU_CONV, LRU_W), LRU_CONV ** -0.5),
        'lru_conv_b': nrm((DEPTH, LRU_W), 0.01),
        'lru_w_a': nrm((DEPTH, LRU_BLOCKS, LRU_BLOCK_W, LRU_BLOCK_W), LRU_BLOCK_W ** -0.5),
        'lru_b_a': nrm((DEPTH, LRU_W), 0.01),
        'lru_w_x': nrm((DEPTH, LRU_BLOCKS, LRU_BLOCK_W, LRU_BLOCK_W), LRU_BLOCK_W ** -0.5),
        'lru_b_x': nrm((DEPTH, LRU_W), 0.01),
        'lru_lambda': jnp.log(u_lam / (1.0 - u_lam)),
        'sc_conv_w': nrm((DEPTH, SC_CONV, SC_W), SC_CONV ** -0.5),
        'w_gate': nrm((DEPTH, D_MODEL, N_BRANCH * D_MODEL), D_MODEL ** -0.5),
        'b_gate': nrm((DEPTH, N_BRANCH * D_MODEL), 0.01),
        'w_br_att': nrm((DEPTH, ATT_W, D_MODEL), ATT_W ** -0.5),
        'w_br_lru': nrm((DEPTH, LRU_W, D_MODEL), LRU_W ** -0.5),
        'w_br_sc': nrm((DEPTH, SC_W, D_MODEL), SC_W ** -0.5),
        'w_out': nrm((DEPTH, D_MODEL, D_MODEL), D_MODEL ** -0.5),
        'norm2_g': 1.0 + nrm((DEPTH, D_MODEL), 0.05),
        'w_ffn_in': nrm((DEPTH, D_MODEL, 2 * D_FF), D_MODEL ** -0.5),
        'w_ffn_out': nrm((DEPTH, D_FF, D_MODEL), D_FF ** -0.5),
        'final_norm_g': 1.0 + nrm((D_MODEL,), 0.05),
    }


def reference(x_prompt, x_sample, cache_k, cache_v, cache_logf, state_lru_h, state_lru_conv,
              state_sc_conv, page_table, norm1_g, w_in, b_f, lru_conv_w, lru_conv_b, lru_w_a,
              lru_b_a, lru_w_x, lru_b_x, lru_lambda, sc_conv_w, w_gate, b_gate, w_br_att,
              w_br_lru, w_br_sc, w_out, norm2_g, w_ffn_in, w_ffn_out, final_norm_g):
    bp = x_prompt.shape[0]
    db, n_pages = page_table.shape
    p_len = n_pages * PAGE_SIZE
    zero_lru_buf = jnp.zeros((bp, LRU_CONV - 1, LRU_W), x_prompt.dtype)
    zero_h = jnp.zeros((bp, LRU_W), jnp.float32)
    zero_sc_buf = jnp.zeros((bp, SC_CONV - 1, SC_W), x_prompt.dtype)

    yp, ys = x_prompt, x_sample
    sp_all, ss_all = [], []
    for l in range(DEPTH):
        w = (norm1_g[l], w_in[l], b_f[l], lru_conv_w[l], lru_conv_b[l], lru_w_a[l], lru_b_a[l],
             lru_w_x[l], lru_b_x[l], lru_lambda[l], sc_conv_w[l], w_gate[l], b_gate[l],
             w_br_att[l], w_br_lru[l], w_br_sc[l], w_out[l], norm2_g[l], w_ffn_in[l], w_ffn_out[l])
        yp, sp = trunk_layer(yp, fox_prompt, zero_lru_buf, zero_h, zero_sc_buf, *w)
        k_past = cache_k[l, page_table].reshape(db, p_len, N_HEADS, HEAD_DIM)
        v_past = cache_v[l, page_table].reshape(db, p_len, N_HEADS, HEAD_DIM)
        lf_past = cache_logf[l, page_table].reshape(db, p_len, N_HEADS)
        attend_s = functools.partial(fox_sample, k_past, v_past, lf_past)
        ys, ss = trunk_layer(ys, attend_s, state_lru_conv[l], state_lru_h[l], state_sc_conv[l], *w)
        sp_all.append(sp)
        ss_all.append(ss)

    y_prompt = rmsnorm(yp, final_norm_g)
    y_sample = rmsnorm(ys, final_norm_g)
    new_k_prompt = jnp.stack([s[0] for s in sp_all])
    new_v_prompt = jnp.stack([s[1] for s in sp_all])
    new_logf_prompt = jnp.stack([s[2] for s in sp_all])
    new_lru_h_prompt = jnp.stack([s[3] for s in sp_all])
    new_lru_conv_prompt = jnp.stack([s[4] for s in sp_all])
    new_sc_conv_prompt = jnp.stack([s[5] for s in sp_all])
    new_k_sample = jnp.stack([s[0] for s in ss_all])
    new_v_sample = jnp.stack([s[1] for s in ss_all])
    new_logf_sample = jnp.stack([s[2] for s in ss_all])
    new_lru_h_sample = jnp.stack([s[3] for s in ss_all])
    new_lru_conv_sample = jnp.stack([s[4] for s in ss_all])
    new_sc_conv_sample = jnp.stack([s[5] for s in ss_all])
    return (y_prompt, y_sample, new_k_prompt, new_v_prompt, new_logf_prompt, new_lru_h_prompt,
            new_lru_conv_prompt, new_sc_conv_prompt, new_k_sample, new_v_sample, new_logf_sample,
            new_lru_h_sample, new_lru_conv_sample, new_sc_conv_sample)
```

```python
import functools

import jax
import jax.numpy as jnp
from jax import lax
from jax.experimental import pallas as pl
from jax.experimental.pallas import tpu as pltpu

N_HEADS = 8
HEAD_DIM = 64
ATT_W = N_HEADS * HEAD_DIM
LRU_W = 512
LRU_BLOCKS = 8
LRU_CONV = 4
LRU_C = 8.0
SC_W = 512
SC_CONV = 3
N_BRANCH = 3
RMS_EPS = 1e-6
NEG_INF = -1e30
PAGE_SIZE = 128

LANES = 128
SUBLANES = 8
VMEM_LIMIT = 60 * 1024 * 1024
ROW_TILE = 512
ATT_BLOCK = 512
FFN_CHUNK = 256
PAGES_PER_STEP = 8

F32 = jnp.float32
BF16 = jnp.bfloat16


def _dot(a, b):
    return jnp.dot(a, b, preferred_element_type=F32)


def _dot_nt(a, b):
    return lax.dot_general(a, b, (((1,), (1,)), ((), ())), preferred_element_type=F32)


def _rmsnorm(x, g):
    ms = jnp.mean(x * x, axis=-1, keepdims=True)
    return x * lax.rsqrt(ms + RMS_EPS) * g


def _softplus(x):
    return jnp.maximum(x, 0.0) + jnp.log1p(jnp.exp(-jnp.abs(x)))


def _row_in_segment(shape, seg):
    return lax.broadcasted_iota(jnp.int32, shape, 0) & (seg - 1)


def _segment_cumsum(x, seg):
    t = _row_in_segment(x.shape, seg)
    s = 1
    while s < seg:
        x = x + jnp.where(t >= s, pltpu.roll(x, s, 0), 0.0)
        s *= 2
    return x


def _segment_linear_scan(a, b, seg):
    t = _row_in_segment(a.shape, seg)
    s = 1
    while s < seg:
        keep = t >= s
        a_prev = jnp.where(keep, pltpu.roll(a, s, 0), 1.0)
        b_prev = jnp.where(keep, pltpu.roll(b, s, 0), 0.0)
        b = a * b_prev + b
        a = a * a_prev
        s *= 2
    return a, b


def _causal_conv(u, hist, w_ref, seg):
    width = w_ref.shape[0]
    t = _row_in_segment(u.shape, seg)
    y = u * w_ref[width - 1:width, :]
    for d in range(1, width):
        shifted = jnp.where(t >= d, pltpu.roll(u, d, 0), pltpu.roll(hist, d, 0))
        y = y + shifted * w_ref[width - 1 - d:width - d, :]
    return y


def _qkv_kernel(x_ref, g_ref, wqkv_ref, wfl_ref, bf_ref,
                q_ref, k_ref, v_ref, kb_ref, vb_ref, lf_ref, fc_ref, carry_ref, *, seg):
    x = x_ref[...]
    xn = _rmsnorm(x, g_ref[...]).astype(BF16)
    z = _dot(xn, wqkv_ref[...])
    q_ref[...] = (z[:, :ATT_W] * (HEAD_DIM ** -0.5)).astype(BF16)
    k = z[:, ATT_W:2 * ATT_W]
    v = z[:, 2 * ATT_W:]
    k_ref[...] = k
    v_ref[...] = v
    kb_ref[...] = k.astype(BF16)
    vb_ref[...] = v.astype(BF16)
    fl = _dot(xn, wfl_ref[...]) + bf_ref[...]
    lf = -_softplus(-fl)
    lf_ref[...] = lf[:, :N_HEADS]

    @pl.when(pl.program_id(1) == 0)
    def _():
        carry_ref[...] = jnp.zeros_like(carry_ref)

    rows = x.shape[0]
    cum = _segment_cumsum(lf, seg)
    if seg == rows:
        cum = cum + carry_ref[SUBLANES - 1:SUBLANES, :]
        carry_ref[...] = cum[rows - SUBLANES:, :]
    fc_ref[...] = cum[:, :N_HEADS]


def _qkv_call(x, g, wqkv, wfl, bfl, rows, seg):
    nb, t_len, d = x.shape
    grid = (nb, t_len // rows)
    row_spec = lambda c: pl.BlockSpec((None, rows, c), lambda b, i: (b, i, 0))
    full = lambda a: pl.BlockSpec(a.shape, lambda b, i: (0,) * a.ndim,
                                  pipeline_mode=pl.Buffered(1))
    out_shape = (
        jax.ShapeDtypeStruct((nb, t_len, ATT_W), BF16),
        jax.ShapeDtypeStruct((nb, t_len, ATT_W), F32),
        jax.ShapeDtypeStruct((nb, t_len, ATT_W), F32),
        jax.ShapeDtypeStruct((nb, t_len, ATT_W), BF16),
        jax.ShapeDtypeStruct((nb, t_len, ATT_W), BF16),
        jax.ShapeDtypeStruct((nb, t_len, N_HEADS), F32),
        jax.ShapeDtypeStruct((nb, t_len, N_HEADS), F32),
    )
    return pl.pallas_call(
        functools.partial(_qkv_kernel, seg=seg),
        grid=grid,
        in_specs=[row_spec(d), full(g), full(wqkv), full(wfl), full(bfl)],
        out_specs=(row_spec(ATT_W),) * 5 + (row_spec(N_HEADS),) * 2,
        out_shape=out_shape,
        scratch_shapes=[pltpu.VMEM((SUBLANES, LANES), F32)],
        compiler_params=pltpu.CompilerParams(
            dimension_semantics=("arbitrary", "arbitrary"), vmem_limit_bytes=VMEM_LIMIT),
        name="qkv",
    )(x, g, wqkv, wfl, bfl)


def _mix_body(x_ref, g_ref, wrest_ref, cw_ref, cb_ref, wa_ref, ba_ref, wx_ref, bx_ref, lam_ref,
              scw_ref, wgate_ref, bgate_ref, wbl_ref, wbs_ref,
              hist_lru, hist_sc, h_in,
              g0_ref, mp_ref, xr_tail_ref, u_tail_ref, h_tail_ref, *, seg):
    x = x_ref[...]
    rows = x.shape[0]
    xn = _rmsnorm(x, g_ref[...]).astype(BF16)
    z = _dot(xn, wrest_ref[...])
    xr = z[:, :LRU_W]
    bg = z[:, LRU_W:LRU_W + SC_W]
    cg = z[:, LRU_W + SC_W:LRU_W + 2 * SC_W]
    hs = z[:, LRU_W + 2 * SC_W:]

    xc = _causal_conv(xr, hist_lru, cw_ref, seg) + cb_ref[...]
    xcb = xc.astype(BF16)
    r = jax.nn.sigmoid(_dot(xcb, wa_ref[...]) + ba_ref[...])
    gate_i = jax.nn.sigmoid(_dot(xcb, wx_ref[...]) + bx_ref[...])
    log_a = (-LRU_C) * r * _softplus(-lam_ref[...])
    a = jnp.exp(log_a)
    mult = jnp.sqrt(-jnp.tanh(log_a) * (a * a + 1.0))
    a_cum, b_cum = _segment_linear_scan(a, mult * gate_i * xc, seg)
    h = a_cum * h_in + b_cum

    u = cg * hs
    sc = bg * _causal_conv(u, hist_sc, scw_ref, seg)

    d = x.shape[1]
    gate = lambda j: jax.nn.sigmoid(
        _dot(xn, wgate_ref[:, j * d:(j + 1) * d]) + bgate_ref[:, j * d:(j + 1) * d])
    g0_ref[...] = gate(0).astype(BF16)
    mp_ref[...] = (gate(1) * _dot(h.astype(BF16), wbl_ref[...])
                   + gate(2) * _dot(sc.astype(BF16), wbs_ref[...]))
    tail = xr_tail_ref.shape[0]
    xr_tail_ref[...] = xr[rows - tail:, :]
    u_tail_ref[...] = u[rows - tail:, :]
    h_tail_ref[...] = h[rows - tail:, :]
    return xr, u, h


def _mix_kernel_fresh(*refs, seg):
    ins, outs, (hist_lru_ref, hist_sc_ref, h_ref) = refs[:15], refs[15:20], refs[20:]

    @pl.when(pl.program_id(1) == 0)
    def _():
        hist_lru_ref[...] = jnp.zeros_like(hist_lru_ref)
        hist_sc_ref[...] = jnp.zeros_like(hist_sc_ref)
        h_ref[...] = jnp.zeros_like(h_ref)

    xr, u, h = _mix_body(*ins, hist_lru_ref[...], hist_sc_ref[...],
                         h_ref[SUBLANES - 1:SUBLANES, :], *outs, seg=seg)
    hist_lru_ref[...] = xr
    hist_sc_ref[...] = u
    h_ref[...] = h[h.shape[0] - SUBLANES:, :]


def _mix_kernel_state(*refs, seg):
    ins, (hist_lru_ref, hist_sc_ref, h_ref), outs = refs[:15], refs[15:18], refs[18:]
    _mix_body(*ins, hist_lru_ref[...], hist_sc_ref[...], h_ref[...], *outs, seg=seg)


def _mix_call(x, weights, rows, seg, state=None):
    nb, t_len, d = x.shape
    grid = (nb, t_len // rows)
    row_spec = lambda c: pl.BlockSpec((None, rows, c), lambda b, i: (b, i, 0))
    full = lambda a: pl.BlockSpec(a.shape, lambda b, i: (0,) * a.ndim,
                                  pipeline_mode=pl.Buffered(1))
    tail = SUBLANES if state is None else rows
    n_tail = 1 if state is None else t_len // rows
    tail_spec = pl.BlockSpec((None, tail, LRU_W), lambda b, i: (b, i if state is not None else 0, 0))
    out_shape = (
        jax.ShapeDtypeStruct((nb, t_len, d), BF16),
        jax.ShapeDtypeStruct((nb, t_len, d), F32),
        jax.ShapeDtypeStruct((nb, n_tail * tail, LRU_W), F32),
        jax.ShapeDtypeStruct((nb, n_tail * tail, SC_W), F32),
        jax.ShapeDtypeStruct((nb, n_tail * tail, LRU_W), F32),
    )
    in_specs = [row_spec(d)] + [full(w) for w in weights]
    args = [x, *weights]
    if state is None:
        kern = functools.partial(_mix_kernel_fresh, seg=seg)
        scratch = [pltpu.VMEM((rows, LRU_W), F32), pltpu.VMEM((rows, SC_W), F32),
                   pltpu.VMEM((SUBLANES, LRU_W), F32)]
    else:
        kern = functools.partial(_mix_kernel_state, seg=seg)
        scratch = []
        in_specs += [row_spec(LRU_W), row_spec(SC_W), row_spec(LRU_W)]
        args += list(state)
    return pl.pallas_call(
        kern,
        grid=grid,
        in_specs=in_specs,
        out_specs=(row_spec(d), row_spec(d), tail_spec, tail_spec, tail_spec),
        out_shape=out_shape,
        scratch_shapes=scratch,
        compiler_params=pltpu.CompilerParams(
            dimension_semantics=("arbitrary", "arbitrary"), vmem_limit_bytes=VMEM_LIMIT),
        name="mix",
    )(*args)


def _flash_kernel(q_ref, k_ref, v_ref, f_ref, o_ref, m_ref, l_ref, acc_ref):
    blk = q_ref.shape[0]
    i = pl.program_id(2)
    q2 = q_ref[...]
    lane = lax.broadcasted_iota(jnp.int32, q2.shape, 1)
    zero = jnp.zeros_like(q2)
    qs = (jnp.where(lane < HEAD_DIM, q2, zero), jnp.where(lane >= HEAD_DIM, q2, zero))

    m_ref[...] = jnp.full_like(m_ref, NEG_INF)
    l_ref[...] = jnp.zeros_like(l_ref)
    acc_ref[...] = jnp.zeros_like(acc_ref)

    def step(j, masked):
        start = pl.multiple_of(j * blk, blk)
        kb = k_ref[pl.ds(start, blk), :]
        vb = v_ref[pl.ds(start, blk), :]
        fj = f_ref[j]
        for hh in range(2):
            s = _dot_nt(qs[hh], kb) - fj[hh:hh + 1, :]
            if masked:
                row = lax.broadcasted_iota(jnp.int32, s.shape, 0)
                col = lax.broadcasted_iota(jnp.int32, s.shape, 1)
                s = jnp.where(row >= col, s, NEG_INF)
            m_old = m_ref[hh]
            m_new = jnp.maximum(m_old, jnp.max(s, axis=1, keepdims=True))
            alpha = jnp.exp(m_old - m_new)
            p_ = jnp.exp(s - m_new)
            l_ref[hh] = alpha * l_ref[hh] + jnp.sum(p_, axis=1, keepdims=True)
            acc_ref[hh] = alpha * acc_ref[hh] + _dot(p_.astype(BF16), vb)
            m_ref[hh] = m_new

    def body(j, carry):
        step(j, False)
        return carry

    lax.fori_loop(0, i, body, 0)
    step(i, True)
    out = jnp.where(lane < HEAD_DIM, acc_ref[0] / l_ref[0], acc_ref[1] / l_ref[1])
    o_ref[...] = out.astype(o_ref.dtype)


def _flash_call(q, k, v, f_blocks, blk):
    nb, s_len, _ = q.shape
    n_pairs = ATT_W // LANES
    grid = (nb, n_pairs, s_len // blk)
    return pl.pallas_call(
        _flash_kernel,
        grid=grid,
        in_specs=[
            pl.BlockSpec((None, blk, LANES), lambda b, p, i: (b, i, p)),
            pl.BlockSpec((None, s_len, LANES), lambda b, p, i: (b, 0, p)),
            pl.BlockSpec((None, s_len, LANES), lambda b, p, i: (b, 0, p)),
            pl.BlockSpec((None, None, s_len // blk, 2, blk), lambda b, p, i: (b, p, 0, 0, 0)),
        ],
        out_specs=pl.BlockSpec((None, blk, LANES), lambda b, p, i: (b, i, p)),
        out_shape=jax.ShapeDtypeStruct((nb, s_len, ATT_W), BF16),
        scratch_shapes=[pltpu.VMEM((2, blk, 1), F32), pltpu.VMEM((2, blk, 1), F32),
                        pltpu.VMEM((2, blk, LANES), F32)],
        compiler_params=pltpu.CompilerParams(
            dimension_semantics=("arbitrary", "arbitrary", "arbitrary"),
            vmem_limit_bytes=VMEM_LIMIT),
        name="flash",
    )(q, k, v, f_blocks)


def _paged_kernel(pt_ref, q_ref, kn_ref, vn_ref, cn_ref, *refs, n_pages):
    del pt_ref
    g_pages = n_pages
    k_refs = refs[:g_pages]
    v_refs = refs[g_pages:2 * g_pages]
    lf_refs = refs[2 * g_pages:3 * g_pages]
    o_ref, m_ref, l_ref, acc_ref, fc_ref = refs[3 * g_pages:]
    j = pl.program_id(1)

    @pl.when(j == 0)
    def _():
        m_ref[...] = jnp.full_like(m_ref, NEG_INF)
        l_ref[...] = jnp.zeros_like(l_ref)
        acc_ref[...] = jnp.zeros_like(acc_ref)
        fc_ref[...] = jnp.zeros_like(fc_ref)

    r_i = lax.broadcasted_iota(jnp.int32, (PAGE_SIZE, PAGE_SIZE), 0)
    c_i = lax.broadcasted_iota(jnp.int32, (PAGE_SIZE, PAGE_SIZE), 1)
    tri = jnp.where(r_i <= c_i, 1.0, 0.0).astype(BF16)

    def page_cumsum(lf, carry):
        hi = lf.astype(BF16)
        lo = (lf - hi.astype(F32)).astype(BF16)
        return _dot(hi, tri) + _dot(lo, tri) + carry

    def update(hh, s_list, v_list):
        m_old = m_ref[hh][:, :1]
        m_new = m_old
        for s in s_list:
            m_new = jnp.maximum(m_new, jnp.max(s, axis=1, keepdims=True))
        alpha = jnp.exp(m_old - m_new)
        l_new = alpha * l_ref[hh][:, :1]
        acc = alpha * acc_ref[hh]
        for s, vt in zip(s_list, v_list):
            p_ = jnp.exp(s - m_new)
            l_new = l_new + jnp.sum(p_, axis=1, keepdims=True)
            acc = acc + _dot_nt(p_.astype(BF16), vt)
        m_ref[hh] = jnp.broadcast_to(m_new, m_ref.shape[1:])
        l_ref[hh] = jnp.broadcast_to(l_new, l_ref.shape[1:])
        acc_ref[hh] = acc

    carry = fc_ref[:, :1]
    cums = []
    for g in range(g_pages):
        cum = page_cumsum(lf_refs[g][...], carry)
        carry = cum[:, PAGE_SIZE - 1:]
        cums.append(cum)
    fc_ref[...] = jnp.broadcast_to(carry, fc_ref.shape)

    for hh in range(N_HEADS):
        qh = q_ref[hh]
        s_list = [_dot(qh, k_refs[g][hh].astype(BF16)) - cums[g][hh:hh + 1, :]
                  for g in range(g_pages)]
        update(hh, s_list, [v_refs[g][hh].astype(BF16) for g in range(g_pages)])

    @pl.when(j == pl.num_programs(1) - 1)
    def _():
        t_len = q_ref.shape[1]
        row = lax.broadcasted_iota(jnp.int32, (t_len, PAGE_SIZE), 0)
        col = lax.broadcasted_iota(jnp.int32, (t_len, PAGE_SIZE), 1)
        f_new = carry + cn_ref[...]
        for hh in range(N_HEADS):
            s = _dot(q_ref[hh], kn_ref[hh]) - f_new[hh:hh + 1, :]
            s = jnp.where(row >= col, s, NEG_INF)
            update(hh, [s], [vn_ref[hh]])
            o_ref[hh] = acc_ref[hh] / l_ref[hh][:, :1]


def _paged_call(layer, page_table, q, k_new_t, v_new_t, cum_new, cache_kt, cache_vt, cache_lft):
    nb, n_pages = page_table.shape
    g_pages = PAGES_PER_STEP
    while n_pages % g_pages:
        g_pages //= 2
    t_len = q.shape[2]
    grid = (nb, n_pages // g_pages)
    per_b = lambda shape: pl.BlockSpec((None,) + shape, lambda b, j, pt: (b,) + (0,) * len(shape))

    def page_spec(shape, g):
        return pl.BlockSpec((None, None) + shape,
                            lambda b, j, pt: (layer, pt[b, j * g_pages + g]) + (0,) * len(shape))

    kv_page = (N_HEADS, HEAD_DIM, PAGE_SIZE)
    in_specs = [per_b((N_HEADS, t_len, HEAD_DIM)), per_b(kv_page), per_b(kv_page),
                per_b((N_HEADS, PAGE_SIZE))]
    in_specs += [page_spec(kv_page, g) for g in range(g_pages)]
    in_specs += [page_spec(kv_page, g) for g in range(g_pages)]
    in_specs += [page_spec((N_HEADS, PAGE_SIZE), g) for g in range(g_pages)]
    grid_spec = pltpu.PrefetchScalarGridSpec(
        num_scalar_prefetch=1,
        grid=grid,
        in_specs=in_specs,
        out_specs=per_b((N_HEADS, t_len, HEAD_DIM)),
        scratch_shapes=[pltpu.VMEM((N_HEADS, t_len, LANES), F32),
                        pltpu.VMEM((N_HEADS, t_len, LANES), F32),
                        pltpu.VMEM((N_HEADS, t_len, HEAD_DIM), F32),
                        pltpu.VMEM((N_HEADS, LANES), F32)],
    )
    return pl.pallas_call(
        functools.partial(_paged_kernel, n_pages=g_pages),
        grid_spec=grid_spec,
        out_shape=jax.ShapeDtypeStruct((nb, N_HEADS, t_len, HEAD_DIM), F32),
        compiler_params=pltpu.CompilerParams(
            dimension_semantics=("arbitrary", "arbitrary"), vmem_limit_bytes=VMEM_LIMIT),
        name="paged",
    )(page_table, q, k_new_t, v_new_t, cum_new,
      *([cache_kt] * g_pages), *([cache_vt] * g_pages), *([cache_lft] * g_pages))


def _post_kernel(x_ref, att_ref, g0_ref, mp_ref, wba_ref, wout_ref, g2_ref, wg_ref, wu_ref,
                 wdown_ref, gf_ref, y_ref, *, final_norm):
    merged = g0_ref[...].astype(F32) * _dot(att_ref[...], wba_ref[...]) + mp_ref[...]
    h = x_ref[...] + _dot(merged.astype(BF16), wout_ref[...])
    hn = _rmsnorm(h, g2_ref[...]).astype(BF16)
    d_ff = wg_ref.shape[1]
    y = h
    for c in range(0, d_ff, FFN_CHUNK):
        g = _dot(hn, wg_ref[:, c:c + FFN_CHUNK])
        u = _dot(hn, wu_ref[:, c:c + FFN_CHUNK])
        act = (g * jax.nn.sigmoid(g) * u).astype(BF16)
        y = y + _dot(act, wdown_ref[c:c + FFN_CHUNK, :])
    if final_norm:
        y = _rmsnorm(y, gf_ref[...])
    y_ref[...] = y


def _post_call(x, att, g0, mpart, weights, rows, final_norm):
    nb, t_len, d = x.shape
    grid = (nb, t_len // rows)
    row_spec = lambda c: pl.BlockSpec((None, rows, c), lambda b, i: (b, i, 0))
    full = lambda a: pl.BlockSpec(a.shape, lambda b, i: (0,) * a.ndim,
                                  pipeline_mode=pl.Buffered(1))
    return pl.pallas_call(
        functools.partial(_post_kernel, final_norm=final_norm),
        grid=grid,
        in_specs=[row_spec(d), row_spec(ATT_W), row_spec(d), row_spec(d)]
                 + [full(w) for w in weights],
        out_specs=row_spec(d),
        out_shape=jax.ShapeDtypeStruct((nb, t_len, d), F32),
        compiler_params=pltpu.CompilerParams(
            dimension_semantics=("arbitrary", "arbitrary"), vmem_limit_bytes=VMEM_LIMIT),
        name="post",
    )(x, att, g0, mpart, *weights)


def _block_diag(w):
    n, d, e = w.shape
    eye = jnp.eye(n, dtype=w.dtype)
    return (w[:, :, None, :] * eye[:, None, :, None]).reshape(n * d, n * e)


def _row2(v):
    return v.reshape(1, -1).astype(F32)


def _history_rows(buf, t_len):
    nb, w1, c = buf.shape
    padded = jnp.concatenate([jnp.zeros((nb, t_len - w1, c), buf.dtype), buf], axis=1)
    return jnp.roll(padded, -1, axis=0).reshape(1, nb * t_len, c)


def kernel(x_prompt, x_sample, cache_k, cache_v, cache_logf, state_lru_h, state_lru_conv,
           state_sc_conv, page_table, norm1_g, w_in, b_f, lru_conv_w, lru_conv_b, lru_w_a,
           lru_b_a, lru_w_x, lru_b_x, lru_lambda, sc_conv_w, w_gate, b_gate, w_br_att,
           w_br_lru, w_br_sc, w_out, norm2_g, w_ffn_in, w_ffn_out, final_norm_g):
    depth = w_in.shape[0]
    bp, s_len, d = x_prompt.shape
    db, t_dec, _ = x_sample.shape
    n_tok_s = db * t_dec
    d_ff = w_ffn_out.shape[1]
    rows_p = min(ROW_TILE, s_len)
    blk = min(ATT_BLOCK, s_len)

    cache_kt = jnp.transpose(cache_k, (0, 1, 3, 4, 2))
    cache_vt = jnp.transpose(cache_v, (0, 1, 3, 4, 2))
    cache_lft = jnp.transpose(cache_logf, (0, 1, 3, 2))

    yp = x_prompt
    ys = x_sample.reshape(1, n_tok_s, d)
    outs_p = [[] for _ in range(6)]
    outs_s = [[] for _ in range(6)]
    q_off = 3 * ATT_W
    r_off = q_off + N_HEADS
    for l in range(depth):
        wi = w_in[l]
        wqkv = wi[:, :q_off].astype(BF16)
        wfl = jnp.pad(wi[:, q_off:r_off], ((0, 0), (0, LANES - N_HEADS))).astype(BF16)
        bfl = jnp.pad(b_f[l].reshape(1, -1), ((0, 0), (0, LANES - N_HEADS))).astype(F32)
        g1 = _row2(norm1_g[l])
        mix_w = [g1, wi[:, r_off:].astype(BF16), lru_conv_w[l], _row2(lru_conv_b[l]),
                 _block_diag(lru_w_a[l]).astype(BF16), _row2(lru_b_a[l]),
                 _block_diag(lru_w_x[l]).astype(BF16), _row2(lru_b_x[l]), _row2(lru_lambda[l]),
                 sc_conv_w[l], w_gate[l].astype(BF16), _row2(b_gate[l]),
                 w_br_lru[l].astype(BF16), w_br_sc[l].astype(BF16)]
        post_w = [w_br_att[l].astype(BF16), w_out[l].astype(BF16), _row2(norm2_g[l]),
                  w_ffn_in[l][:, :d_ff].astype(BF16), w_ffn_in[l][:, d_ff:].astype(BF16),
                  w_ffn_out[l].astype(BF16), _row2(final_norm_g)]
        last = l == depth - 1

        q, k, v, kb, vb, lf, fc = _qkv_call(yp, g1, wqkv, wfl, bfl, rows_p, rows_p)
        g0, mpart, xr_t, u_t, h_t = _mix_call(yp, mix_w, rows_p, rows_p)
        f_blocks = jnp.transpose(fc.reshape(bp, s_len // blk, blk, N_HEADS // 2, 2),
                                 (0, 3, 1, 4, 2))
        att = _flash_call(q, kb, vb, f_blocks, blk)
        yp = _post_call(yp, att, g0, mpart, post_w, rows_p, last)
        outs_p[0].append(k.reshape(bp, s_len, N_HEADS, HEAD_DIM))
        outs_p[1].append(v.reshape(bp, s_len, N_HEADS, HEAD_DIM))
        outs_p[2].append(lf)
        outs_p[3].append(h_t[:, SUBLANES - 1])
        outs_p[4].append(xr_t[:, SUBLANES - (LRU_CONV - 1):])
        outs_p[5].append(u_t[:, SUBLANES - (SC_CONV - 1):])

        q, k, v, kb, vb, lf, fc = _qkv_call(ys, g1, wqkv, wfl, bfl, n_tok_s, t_dec)
        state = (_history_rows(state_lru_conv[l], t_dec), _history_rows(state_sc_conv[l], t_dec),
                 jnp.repeat(state_lru_h[l], t_dec, axis=0).reshape(1, n_tok_s, LRU_W))
        g0, mpart, xr_t, u_t, h_t = _mix_call(ys, mix_w, n_tok_s, t_dec, state)
        heads = lambda a: a.reshape(db, t_dec, N_HEADS, HEAD_DIM)
        pad_pos = lambda a: jnp.pad(a, [(0, 0)] * (a.ndim - 1) + [(0, PAGE_SIZE - t_dec)])
        q_h = jnp.transpose(heads(q), (0, 2, 1, 3))
        k_new_t = pad_pos(jnp.transpose(heads(kb), (0, 2, 3, 1)))
        v_new_t = pad_pos(jnp.transpose(heads(vb), (0, 2, 3, 1)))
        cum_new = pad_pos(jnp.transpose(fc.reshape(db, t_dec, N_HEADS), (0, 2, 1)))
        att = _paged_call(l, page_table, q_h, k_new_t, v_new_t, cum_new,
                          cache_kt, cache_vt, cache_lft)
        att = jnp.transpose(att, (0, 2, 1, 3)).reshape(1, n_tok_s, ATT_W).astype(BF16)
        ys = _post_call(ys, att, g0, mpart, post_w, n_tok_s, last)
        outs_s[0].append(k.reshape(db, t_dec, N_HEADS, HEAD_DIM))
        outs_s[1].append(v.reshape(db, t_dec, N_HEADS, HEAD_DIM))
        outs_s[2].append(lf.reshape(db, t_dec, N_HEADS))
        outs_s[3].append(h_t.reshape(db, t_dec, LRU_W)[:, t_dec - 1])
        outs_s[4].append(xr_t.reshape(db, t_dec, LRU_W)[:, t_dec - (LRU_CONV - 1):])
        outs_s[5].append(u_t.reshape(db, t_dec, SC_W)[:, t_dec - (SC_CONV - 1):])

    return (yp, ys.reshape(db, t_dec, d),
            *[jnp.stack(o) for o in outs_p], *[jnp.stack(o) for o in outs_s])
```

```python
import functools
import math

import numpy as np
import jax
import jax.numpy as jnp
from jax import lax
from jax.experimental import pallas as pl
from jax.experimental.pallas import tpu as pltpu

N_HEADS = 8
HEAD_DIM = 64
ATT_W = N_HEADS * HEAD_DIM
LRU_W = 512
LRU_BLOCKS = 8
LRU_CONV = 4
LRU_C = 8.0
SC_W = 512
SC_CONV = 3
N_BRANCH = 3
RMS_EPS = 1e-6
NEG_INF = -1e30
PAGE_SIZE = 128
LOG2E = math.log2(math.e)

LANES = 128
SUBLANES = 8
BF16_ROWS = 16
VMEM_LIMIT = 60 * 1024 * 1024
ROW_TILE = 512
ATT_BLOCK = ROW_TILE
FFN_CHUNK = 256
PAGES_PER_STEP = 16
N_PAIRS = ATT_W // LANES
F_PARTS = 3
VT_ROWS = LANES + BF16_ROWS
PAIRS_PER_STEP = 2
KEY_SUB_BLOCK = 256
PAGED_K = ATT_W + LANES

F32 = jnp.float32
BF16 = jnp.bfloat16


def _dot(a, b):
    return jnp.dot(a, b, preferred_element_type=F32)


def _dot_nt(a, b):
    return lax.dot_general(a, b, (((1,), (1,)), ((), ())), preferred_element_type=F32)


def _rmsnorm(x, g):
    ms = jnp.mean(x * x, axis=-1, keepdims=True)
    return x * lax.rsqrt(ms + RMS_EPS) * g


def _softplus(x):
    return jnp.maximum(x, 0.0) + jnp.log1p(jnp.exp(-jnp.abs(x)))


def _bf16_pieces(x, n):
    pieces = []
    for _ in range(n - 1):
        p = x.astype(BF16)
        pieces.append(p)
        x = x - p.astype(F32)
    pieces.append(x.astype(BF16))
    return pieces


def _row_in_segment(shape, seg):
    return lax.broadcasted_iota(jnp.int32, shape, 0) & (seg - 1)


def _segment_cumsum(x, seg):
    t = _row_in_segment(x.shape, seg)
    s = 1
    while s < seg:
        x = x + jnp.where(t >= s, pltpu.roll(x, s, 0), 0.0)
        s *= 2
    return x


def _segment_linear_scan(a, b, seg):
    t = _row_in_segment(a.shape, seg)
    s = 1
    while s < seg:
        keep = t >= s
        a_prev = jnp.where(keep, pltpu.roll(a, s, 0), 1.0)
        b_prev = jnp.where(keep, pltpu.roll(b, s, 0), 0.0)
        b = a * b_prev + b
        a = a * a_prev
        s *= 2
    return a, b


def _causal_conv(u, hist, w_ref, seg):
    width = w_ref.shape[0]
    t = _row_in_segment(u.shape, seg)
    y = u * w_ref[width - 1:width, :]
    for d in range(1, width):
        shifted = jnp.where(t >= d, pltpu.roll(u, d, 0), pltpu.roll(hist, d, 0))
        y = y + shifted * w_ref[width - 1 - d:width - d, :]
    return y


def _qkv_project(x_ref, g_ref, wqkv_ref, wfl_ref, bf_ref):
    xn = _rmsnorm(x_ref[...], g_ref[...]).astype(BF16)
    z = _dot(xn, wqkv_ref[...])
    fl = _dot(xn, wfl_ref[...]) + bf_ref[...]
    lf = -_softplus(-fl)
    return z[:, :ATT_W], z[:, ATT_W:2 * ATT_W], z[:, 2 * ATT_W:], lf


def _qkv_prompt_kernel(x_ref, g_ref, wqkv_ref, wfl_ref, bf_ref, place_ref,
                       q_ref, k2_ref, vt_ref, ktf_ref, vtf_ref, lft_ref, carry_ref):
    rows = x_ref.shape[0]
    q, k, v, lf = _qkv_project(x_ref, g_ref, wqkv_ref, wfl_ref, bf_ref)
    q_ref[...] = (q * (HEAD_DIM ** -0.5 * LOG2E)).astype(BF16)
    ktf_ref[...] = k.T
    vt = v.T
    vtf_ref[...] = vt
    vtb = vt.astype(BF16)
    ones = jnp.ones((BF16_ROWS, rows), BF16)
    for p in range(N_PAIRS):
        vt_ref[p] = jnp.concatenate([vtb[p * LANES:(p + 1) * LANES, :], ones], axis=0)
    lft_ref[...] = lf.T[:N_HEADS, :]

    @pl.when(pl.program_id(1) == 0)
    def _():
        carry_ref[...] = jnp.zeros_like(carry_ref)

    cum = _segment_cumsum(lf, rows) + carry_ref[SUBLANES - 1:SUBLANES, :]
    carry_ref[...] = cum[rows - SUBLANES:, :]
    pieces = jnp.concatenate(_bf16_pieces(cum * LOG2E, F_PARTS), axis=1)
    placed = _dot(pieces, place_ref[...]).astype(BF16)
    kb = k.astype(BF16)
    for p in range(N_PAIRS):
        k2_ref[:, 2 * p * LANES:(2 * p + 1) * LANES] = kb[:, p * LANES:(p + 1) * LANES]
        k2_ref[:, (2 * p + 1) * LANES:(2 * p + 2) * LANES] = placed[:, p * LANES:(p + 1) * LANES]


def _qkv_prompt_call(x, g, wqkv, wfl, bfl, place, rows):
    nb, s_len, d = x.shape
    n_t = s_len // rows
    row_spec = lambda c: pl.BlockSpec((None, rows, c), lambda b, i: (b, i, 0))
    col_spec = lambda r: pl.BlockSpec((None, r, rows), lambda b, i: (b, 0, i))
    full = lambda a: pl.BlockSpec(a.shape, lambda b, i: (0,) * a.ndim,
                                  pipeline_mode=pl.Buffered(1))
    out_shape = (
        jax.ShapeDtypeStruct((nb, s_len, ATT_W), BF16),
        jax.ShapeDtypeStruct((nb, s_len, 2 * ATT_W), BF16),
        jax.ShapeDtypeStruct((nb, N_PAIRS, n_t, VT_ROWS, rows), BF16),
        jax.ShapeDtypeStruct((nb, ATT_W, s_len), F32),
        jax.ShapeDtypeStruct((nb, ATT_W, s_len), F32),
        jax.ShapeDtypeStruct((nb, N_HEADS, s_len), F32),
    )
    return pl.pallas_call(
        _qkv_prompt_kernel,
        grid=(nb, n_t),
        in_specs=[row_spec(d), full(g), full(wqkv), full(wfl), full(bfl), full(place)],
        out_specs=(row_spec(ATT_W), row_spec(2 * ATT_W),
                   pl.BlockSpec((None, N_PAIRS, None, VT_ROWS, rows), lambda b, i: (b, 0, i, 0, 0)),
                   col_spec(ATT_W), col_spec(ATT_W), col_spec(N_HEADS)),
        out_shape=out_shape,
        scratch_shapes=[pltpu.VMEM((SUBLANES, LANES), F32)],
        compiler_params=pltpu.CompilerParams(
            dimension_semantics=("arbitrary", "arbitrary"), vmem_limit_bytes=VMEM_LIMIT),
        name="qkv_prompt",
    )(x, g, wqkv, wfl, bfl, place)


def _qkv_sample_kernel(x_ref, g_ref, wqkv_ref, wfl_ref, bf_ref,
                       q_ref, k_ref, v_ref, kb_ref, vb_ref, lf_ref, fc_ref, *, seg):
    q, k, v, lf = _qkv_project(x_ref, g_ref, wqkv_ref, wfl_ref, bf_ref)
    q_ref[...] = (q * (HEAD_DIM ** -0.5)).astype(BF16)
    k_ref[...] = k
    v_ref[...] = v
    kb_ref[...] = k.astype(BF16)
    vb_ref[...] = v.astype(BF16)
    lf_ref[...] = lf[:, :N_HEADS]
    fc_ref[...] = _segment_cumsum(lf, seg)[:, :N_HEADS]


def _qkv_sample_call(x, g, wqkv, wfl, bfl, seg):
    rows, d = x.shape
    full = lambda a: pl.BlockSpec(a.shape, lambda i: (0,) * a.ndim)
    sds = lambda c, dt: jax.ShapeDtypeStruct((rows, c), dt)
    out_shape = (sds(ATT_W, BF16), sds(ATT_W, F32), sds(ATT_W, F32), sds(ATT_W, BF16),
                 sds(ATT_W, BF16), sds(N_HEADS, F32), sds(N_HEADS, F32))
    return pl.pallas_call(
        functools.partial(_qkv_sample_kernel, seg=seg),
        grid=(1,),
        in_specs=[full(x), full(g), full(wqkv), full(wfl), full(bfl)],
        out_specs=tuple(full(o) for o in out_shape),
        out_shape=out_shape,
        compiler_params=pltpu.CompilerParams(
            dimension_semantics=("arbitrary",), vmem_limit_bytes=VMEM_LIMIT),
        name="qkv_sample",
    )(x, g, wqkv, wfl, bfl)


def _mix_body(x_ref, g_ref, wrest_ref, cw_ref, cb_ref, wa_ref, ba_ref, wx_ref, bx_ref, lam_ref,
              scw_ref, wgate_ref, bgate_ref, wbl_ref, wbs_ref,
              hist_lru, hist_sc, h_in,
              g0_ref, mp_ref, xr_tail_ref, u_tail_ref, h_tail_ref, *, seg):
    x = x_ref[...]
    rows = x.shape[0]
    xn = _rmsnorm(x, g_ref[...]).astype(BF16)
    z = _dot(xn, wrest_ref[...])
    xr = z[:, :LRU_W]
    bg = z[:, LRU_W:LRU_W + SC_W]
    cg = z[:, LRU_W + SC_W:LRU_W + 2 * SC_W]
    hs = z[:, LRU_W + 2 * SC_W:]

    xc = _causal_conv(xr, hist_lru, cw_ref, seg) + cb_ref[...]
    xcb = xc.astype(BF16)
    r = jax.nn.sigmoid(_dot(xcb, wa_ref[...]) + ba_ref[...])
    gate_i = jax.nn.sigmoid(_dot(xcb, wx_ref[...]) + bx_ref[...])
    log_a = (-LRU_C) * r * _softplus(-lam_ref[...])
    a = jnp.exp(log_a)
    mult = jnp.sqrt(-jnp.tanh(log_a) * (a * a + 1.0))
    a_cum, b_cum = _segment_linear_scan(a, mult * gate_i * xc, seg)
    h = a_cum * h_in + b_cum

    u = cg * hs
    sc = bg * _causal_conv(u, hist_sc, scw_ref, seg)

    d = x.shape[1]
    gate = lambda j: jax.nn.sigmoid(
        _dot(xn, wgate_ref[:, j * d:(j + 1) * d]) + bgate_ref[:, j * d:(j + 1) * d])
    g0_ref[...] = gate(0).astype(BF16)
    mp_ref[...] = (gate(1) * _dot(h.astype(BF16), wbl_ref[...])
                   + gate(2) * _dot(sc.astype(BF16), wbs_ref[...]))
    tail = xr_tail_ref.shape[0]
    xr_tail_ref[...] = xr[rows - tail:, :]
    u_tail_ref[...] = u[rows - tail:, :]
    h_tail_ref[...] = h[rows - tail:, :]
    return xr, u, h


def _mix_kernel_fresh(*refs, seg):
    ins, outs, (hist_lru_ref, hist_sc_ref, h_ref) = refs[:15], refs[15:20], refs[20:]

    @pl.when(pl.program_id(1) == 0)
    def _():
        hist_lru_ref[...] = jnp.zeros_like(hist_lru_ref)
        hist_sc_ref[...] = jnp.zeros_like(hist_sc_ref)
        h_ref[...] = jnp.zeros_like(h_ref)

    xr, u, h = _mix_body(*ins, hist_lru_ref[...], hist_sc_ref[...],
                         h_ref[SUBLANES - 1:SUBLANES, :], *outs, seg=seg)
    hist_lru_ref[...] = xr
    hist_sc_ref[...] = u
    h_ref[...] = h[h.shape[0] - SUBLANES:, :]


def _mix_kernel_state(*refs, seg):
    ins, (hist_lru_ref, hist_sc_ref, h_ref), outs = refs[:15], refs[15:18], refs[18:]
    _mix_body(*ins, hist_lru_ref[...], hist_sc_ref[...], h_ref[...], *outs, seg=seg)


def _mix_call(x, weights, rows, seg, state=None):
    nb, t_len, d = x.shape
    grid = (nb, t_len // rows)
    row_spec = lambda c: pl.BlockSpec((None, rows, c), lambda b, i: (b, i, 0))
    full = lambda a: pl.BlockSpec(a.shape, lambda b, i: (0,) * a.ndim,
                                  pipeline_mode=pl.Buffered(1))
    tail = SUBLANES if state is None else rows
    n_tail = 1 if state is None else t_len // rows
    tail_spec = pl.BlockSpec((None, tail, LRU_W), lambda b, i: (b, i if state is not None else 0, 0))
    out_shape = (
        jax.ShapeDtypeStruct((nb, t_len, d), BF16),
        jax.ShapeDtypeStruct((nb, t_len, d), F32),
        jax.ShapeDtypeStruct((nb, n_tail * tail, LRU_W), F32),
        jax.ShapeDtypeStruct((nb, n_tail * tail, SC_W), F32),
        jax.ShapeDtypeStruct((nb, n_tail * tail, LRU_W), F32),
    )
    in_specs = [row_spec(d)] + [full(w) for w in weights]
    args = [x, *weights]
    if state is None:
        kern = functools.partial(_mix_kernel_fresh, seg=seg)
        scratch = [pltpu.VMEM((rows, LRU_W), F32), pltpu.VMEM((rows, SC_W), F32),
                   pltpu.VMEM((SUBLANES, LRU_W), F32)]
    else:
        kern = functools.partial(_mix_kernel_state, seg=seg)
        scratch = []
        in_specs += [row_spec(LRU_W), row_spec(SC_W), row_spec(LRU_W)]
        args += list(state)
    return pl.pallas_call(
        kern,
        grid=grid,
        in_specs=in_specs,
        out_specs=(row_spec(d), row_spec(d), tail_spec, tail_spec, tail_spec),
        out_shape=out_shape,
        scratch_shapes=scratch,
        compiler_params=pltpu.CompilerParams(
            dimension_semantics=("arbitrary", "arbitrary"), vmem_limit_bytes=VMEM_LIMIT),
        name="mix",
    )(*args)


def _flash_kernel(q_ref, k2_ref, vt_ref, sel_ref, o_ref, m_ref, acc_ref):
    blk = q_ref.shape[0]
    n_chain = vt_ref.shape[0]
    i = pl.program_id(2)
    sel = sel_ref[...]
    qabs = []
    for c in range(n_chain):
        q2 = q_ref[:, c * LANES:(c + 1) * LANES].astype(F32)
        f_lanes = lambda e: jnp.broadcast_to(sel[2 + e:3 + e, :], q2.shape)
        qabs.append(jnp.concatenate(
            [jnp.concatenate([q2 * sel[e:e + 1, :], f_lanes(e)], axis=1) for e in range(2)],
            axis=0).astype(BF16))

    m_ref[...] = jnp.full_like(m_ref, NEG_INF)
    acc_ref[...] = jnp.zeros_like(acc_ref)

    def step(j, masked):
        start = pl.multiple_of(j * blk, blk)
        sub = min(KEY_SUB_BLOCK, blk)
        scores = [[_dot_nt(k2_ref[pl.ds(start + s0, sub), 2 * c * LANES:2 * (c + 1) * LANES],
                           qabs[c]) for s0 in range(0, blk, sub)] for c in range(n_chain)]
        for c in range(n_chain):
            m_run = m_ref[c]
            for si, s0 in enumerate(range(0, blk, sub)):
                st = scores[c][si]
                if masked:
                    key = lax.broadcasted_iota(jnp.int32, st.shape, 0) + s0
                    qry = lax.broadcasted_iota(jnp.int32, st.shape, 1) & (blk - 1)
                    st = jnp.where(key <= qry, st, NEG_INF)
                m_new = jnp.maximum(m_run, jnp.max(st, axis=0, keepdims=True))
                alpha = jnp.exp2(m_run - m_new)
                p_ = jnp.exp2((st - m_new).astype(BF16))
                acc_ref[c] = alpha * acc_ref[c] + _dot(vt_ref[c, j, :, s0:s0 + sub], p_)
                m_run = m_new
            m_ref[c] = m_run

    def body(j, carry):
        step(j, False)
        return carry

    lax.fori_loop(0, i, body, 0)
    step(i, True)
    row = lax.broadcasted_iota(jnp.int32, (LANES, blk), 0)
    for c in range(n_chain):
        acc = acc_ref[c]
        out = acc[:LANES, :] / acc[LANES:LANES + 1, :]
        out = jnp.where(row < HEAD_DIM, out[:, :blk], out[:, blk:])
        o_ref[:, c * LANES:(c + 1) * LANES] = out.T.astype(o_ref.dtype)


def _flash_call(q, k2, vt):
    nb, s_len, _ = q.shape
    blk = vt.shape[-1]
    n_chain = PAIRS_PER_STEP
    grid = (nb, N_PAIRS // n_chain, s_len // blk)
    sel = np.zeros((SUBLANES, LANES), np.float32)
    sel[0, :HEAD_DIM] = 1.0
    sel[1, HEAD_DIM:] = 1.0
    sel[2, :F_PARTS] = -1.0
    sel[3, F_PARTS:2 * F_PARTS] = -1.0
    sel = jnp.asarray(sel)
    return pl.pallas_call(
        _flash_kernel,
        grid=grid,
        in_specs=[
            pl.BlockSpec((None, blk, n_chain * LANES), lambda b, p, i: (b, i, p)),
            pl.BlockSpec((None, s_len, 2 * n_chain * LANES), lambda b, p, i: (b, 0, p)),
            pl.BlockSpec((None, n_chain, s_len // blk, VT_ROWS, blk),
                         lambda b, p, i: (b, p, 0, 0, 0)),
            pl.BlockSpec(sel.shape, lambda b, p, i: (0, 0)),
        ],
        out_specs=pl.BlockSpec((None, blk, n_chain * LANES), lambda b, p, i: (b, i, p)),
        out_shape=jax.ShapeDtypeStruct((nb, s_len, ATT_W), BF16),
        scratch_shapes=[pltpu.VMEM((n_chain, 1, 2 * blk), F32),
                        pltpu.VMEM((n_chain, VT_ROWS, 2 * blk), F32)],
        compiler_params=pltpu.CompilerParams(
            dimension_semantics=("arbitrary", "arbitrary", "arbitrary"),
            vmem_limit_bytes=VMEM_LIMIT),
        name="flash",
    )(q, k2, vt, sel)


def _paged_kernel(pt_ref, qa_ref, kn_ref, vn_ref, cn_ref, *refs, n_pages):
    del pt_ref
    k_refs = refs[:n_pages]
    v_refs = refs[n_pages:2 * n_pages]
    lf_refs = refs[2 * n_pages:3 * n_pages]
    o_ref, m_ref, l_ref, acc_ref, fc_ref = refs[3 * n_pages:]
    j = pl.program_id(1)
    n_rows = qa_ref.shape[0]
    t_len = n_rows // N_HEADS

    @pl.when(j == 0)
    def _():
        m_ref[...] = jnp.full_like(m_ref, NEG_INF)
        l_ref[...] = jnp.zeros_like(l_ref)
        acc_ref[...] = jnp.zeros_like(acc_ref)
        fc_ref[...] = jnp.zeros_like(fc_ref)

    qa = qa_ref[...]
    pad = jnp.zeros((PAGED_K - ATT_W - BF16_ROWS, PAGE_SIZE), BF16)

    def scores(k_page, cum):
        hi, lo = _bf16_pieces(cum, 2)
        pieces = jnp.concatenate([hi.astype(F32), lo.astype(F32)], axis=0).astype(BF16)
        return _dot(qa, jnp.concatenate([k_page, pieces, pad], axis=0))

    def update(s_list, v_list):
        s_all = jnp.concatenate(s_list, axis=1)
        v_all = jnp.concatenate(v_list, axis=1)
        m_old = m_ref[...][:, :1]
        m_new = jnp.maximum(m_old, jnp.max(s_all, axis=1, keepdims=True))
        alpha = jnp.exp(m_old - m_new)
        p_ = jnp.exp(s_all - m_new)
        l_new = alpha * l_ref[...][:, :1] + jnp.sum(p_, axis=1, keepdims=True)
        acc_ref[...] = alpha * acc_ref[...] + _dot_nt(p_.astype(BF16), v_all)
        m_ref[...] = jnp.broadcast_to(m_new, m_ref.shape)
        l_ref[...] = jnp.broadcast_to(l_new, l_ref.shape)

    r_i = lax.broadcasted_iota(jnp.int32, (PAGE_SIZE, PAGE_SIZE), 0)
    c_i = lax.broadcasted_iota(jnp.int32, (PAGE_SIZE, PAGE_SIZE), 1)
    tri = jnp.where(r_i <= c_i, 1.0, 0.0).astype(BF16)
    lf_pages = [r[...] for r in lf_refs]
    lf_hi, lf_lo = _bf16_pieces(jnp.concatenate(lf_pages, axis=0), 2)
    cum_local = _dot(lf_hi, tri) + _dot(lf_lo, tri)

    carry = fc_ref[:, :1]
    s_list = []
    for g in range(n_pages):
        cum = cum_local[g * N_HEADS:(g + 1) * N_HEADS, :] + carry
        carry = carry + jnp.sum(lf_pages[g], axis=1, keepdims=True)
        s_list.append(scores(k_refs[g][...].astype(BF16), cum))
    fc_ref[...] = jnp.broadcast_to(carry, fc_ref.shape)
    update(s_list, [r[...].astype(BF16) for r in v_refs])

    @pl.when(j == pl.num_programs(1) - 1)
    def _():
        s = scores(kn_ref[...], carry + cn_ref[...])
        qry = lax.broadcasted_iota(jnp.int32, s.shape, 0) & (t_len - 1)
        pos = lax.broadcasted_iota(jnp.int32, s.shape, 1)
        update([jnp.where(pos <= qry, s, NEG_INF)], [vn_ref[...]])
        out = acc_ref[...] / l_ref[...][:, :1]
        for h in range(N_HEADS):
            o_ref[:, h * HEAD_DIM:(h + 1) * HEAD_DIM] = (
                out[h * t_len:(h + 1) * t_len, h * HEAD_DIM:(h + 1) * HEAD_DIM])


def _paged_call(layer, page_table, qa, k_new, v_new, cum_new, cache_kt, cache_vt, cache_lft):
    nb, n_pages = page_table.shape
    g_pages = PAGES_PER_STEP
    while n_pages % g_pages:
        g_pages //= 2
    n_rows = qa.shape[1]
    t_len = n_rows // N_HEADS
    grid = (nb, n_pages // g_pages)
    per_b = lambda shape: pl.BlockSpec((None,) + shape, lambda b, j, pt: (b,) + (0,) * len(shape))

    def page_spec(shape, g):
        return pl.BlockSpec((None, None) + shape,
                            lambda b, j, pt: (layer, pt[b, j * g_pages + g]) + (0,) * len(shape))

    kv_page = (ATT_W, PAGE_SIZE)
    in_specs = [per_b((n_rows, PAGED_K)), per_b(kv_page), per_b(kv_page),
                per_b((N_HEADS, PAGE_SIZE))]
    in_specs += [page_spec(kv_page, g) for g in range(g_pages)]
    in_specs += [page_spec(kv_page, g) for g in range(g_pages)]
    in_specs += [page_spec((N_HEADS, PAGE_SIZE), g) for g in range(g_pages)]
    grid_spec = pltpu.PrefetchScalarGridSpec(
        num_scalar_prefetch=1,
        grid=grid,
        in_specs=in_specs,
        out_specs=per_b((t_len, ATT_W)),
        scratch_shapes=[pltpu.VMEM((n_rows, LANES), F32), pltpu.VMEM((n_rows, LANES), F32),
                        pltpu.VMEM((n_rows, ATT_W), F32), pltpu.VMEM((N_HEADS, LANES), F32)],
    )
    return pl.pallas_call(
        functools.partial(_paged_kernel, n_pages=g_pages),
        grid_spec=grid_spec,
        out_shape=jax.ShapeDtypeStruct((nb, t_len, ATT_W), F32),
        compiler_params=pltpu.CompilerParams(
            dimension_semantics=("arbitrary", "arbitrary"), vmem_limit_bytes=VMEM_LIMIT),
        name="paged",
    )(page_table, qa, k_new, v_new, cum_new,
      *([cache_kt] * g_pages), *([cache_vt] * g_pages), *([cache_lft] * g_pages))


def _post_kernel(x_ref, att_ref, g0_ref, mp_ref, wba_ref, wout_ref, g2_ref, wg_ref, wu_ref,
                 wdown_ref, gf_ref, y_ref, *, final_norm):
    merged = g0_ref[...].astype(F32) * _dot(att_ref[...], wba_ref[...]) + mp_ref[...]
    h = x_ref[...] + _dot(merged.astype(BF16), wout_ref[...])
    hn = _rmsnorm(h, g2_ref[...]).astype(BF16)
    d_ff = wg_ref.shape[1]
    y = h
    for c in range(0, d_ff, FFN_CHUNK):
        g = _dot(hn, wg_ref[:, c:c + FFN_CHUNK])
        u = _dot(hn, wu_ref[:, c:c + FFN_CHUNK])
        act = (g * jax.nn.sigmoid(g) * u).astype(BF16)
        y = y + _dot(act, wdown_ref[c:c + FFN_CHUNK, :])
    if final_norm:
        y = _rmsnorm(y, gf_ref[...])
    y_ref[...] = y


def _post_call(x, att, g0, mpart, weights, rows, final_norm):
    nb, t_len, d = x.shape
    grid = (nb, t_len // rows)
    row_spec = lambda c: pl.BlockSpec((None, rows, c), lambda b, i: (b, i, 0))
    full = lambda a: pl.BlockSpec(a.shape, lambda b, i: (0,) * a.ndim,
                                  pipeline_mode=pl.Buffered(1))
    return pl.pallas_call(
        functools.partial(_post_kernel, final_norm=final_norm),
        grid=grid,
        in_specs=[row_spec(d), row_spec(ATT_W), row_spec(d), row_spec(d)]
                 + [full(w) for w in weights],
        out_specs=row_spec(d),
        out_shape=jax.ShapeDtypeStruct((nb, t_len, d), F32),
        compiler_params=pltpu.CompilerParams(
            dimension_semantics=("arbitrary", "arbitrary"), vmem_limit_bytes=VMEM_LIMIT),
        name="post",
    )(x, att, g0, mpart, *weights)


def _block_diag(w):
    n, d, e = w.shape
    eye = jnp.eye(n, dtype=w.dtype)
    return (w[:, :, None, :] * eye[:, None, :, None]).reshape(n * d, n * e)


def _row2(v):
    return v.reshape(1, -1).astype(F32)


def _history_rows(buf, t_len):
    nb, w1, c = buf.shape
    padded = jnp.concatenate([jnp.zeros((nb, t_len - w1, c), buf.dtype), buf], axis=1)
    return jnp.roll(padded, -1, axis=0).reshape(1, nb * t_len, c)


def _f_placement():
    place = np.zeros((F_PARTS * LANES, N_PAIRS * LANES), np.float32)
    for k in range(F_PARTS):
        for h in range(N_HEADS):
            place[k * LANES + h, (h // 2) * LANES + F_PARTS * (h % 2) + k] = 1.0
    return jnp.asarray(place, BF16)


def _paged_queries(q, db, t_dec):
    q4 = jnp.transpose(q.reshape(db, t_dec, N_HEADS, HEAD_DIM), (0, 2, 1, 3))
    eye = jnp.eye(N_HEADS, dtype=q.dtype)
    q_bd = (q4[:, :, :, None, :] * eye[None, :, None, :, None]).reshape(
        db, N_HEADS * t_dec, ATT_W)
    sel = -jnp.repeat(eye, t_dec, axis=0)
    aug = jnp.concatenate(
        [sel, sel, jnp.zeros((N_HEADS * t_dec, PAGED_K - ATT_W - 2 * N_HEADS), q.dtype)], axis=1)
    return jnp.concatenate([q_bd, jnp.broadcast_to(aug, (db,) + aug.shape)], axis=2)


def kernel(x_prompt, x_sample, cache_k, cache_v, cache_logf, state_lru_h, state_lru_conv,
           state_sc_conv, page_table, norm1_g, w_in, b_f, lru_conv_w, lru_conv_b, lru_w_a,
           lru_b_a, lru_w_x, lru_b_x, lru_lambda, sc_conv_w, w_gate, b_gate, w_br_att,
           w_br_lru, w_br_sc, w_out, norm2_g, w_ffn_in, w_ffn_out, final_norm_g):
    depth = w_in.shape[0]
    bp, s_len, d = x_prompt.shape
    db, t_dec, _ = x_sample.shape
    n_tok_s = db * t_dec
    d_ff = w_ffn_out.shape[1]
    rows_p = min(ROW_TILE, s_len)
    n_pool = cache_k.shape[1]

    cache_kt = jnp.transpose(cache_k, (0, 1, 3, 4, 2)).reshape(depth, n_pool, ATT_W, PAGE_SIZE)
    cache_vt = jnp.transpose(cache_v, (0, 1, 3, 4, 2)).reshape(depth, n_pool, ATT_W, PAGE_SIZE)
    cache_lft = jnp.transpose(cache_logf, (0, 1, 3, 2))
    place = _f_placement()

    yp = x_prompt
    ys = x_sample.reshape(1, n_tok_s, d)
    outs_p = [[] for _ in range(6)]
    outs_s = [[] for _ in range(6)]
    q_off = 3 * ATT_W
    r_off = q_off + N_HEADS
    for l in range(depth):
        wi = w_in[l]
        wqkv = wi[:, :q_off].astype(BF16)
        wfl = jnp.pad(wi[:, q_off:r_off], ((0, 0), (0, LANES - N_HEADS))).astype(BF16)
        bfl = jnp.pad(b_f[l].reshape(1, -1), ((0, 0), (0, LANES - N_HEADS))).astype(F32)
        g1 = _row2(norm1_g[l])
        mix_w = [g1, wi[:, r_off:].astype(BF16), lru_conv_w[l], _row2(lru_conv_b[l]),
                 _block_diag(lru_w_a[l]).astype(BF16), _row2(lru_b_a[l]),
                 _block_diag(lru_w_x[l]).astype(BF16), _row2(lru_b_x[l]), _row2(lru_lambda[l]),
                 sc_conv_w[l], w_gate[l].astype(BF16), _row2(b_gate[l]),
                 w_br_lru[l].astype(BF16), w_br_sc[l].astype(BF16)]
        post_w = [w_br_att[l].astype(BF16), w_out[l].astype(BF16), _row2(norm2_g[l]),
                  w_ffn_in[l][:, :d_ff].astype(BF16), w_ffn_in[l][:, d_ff:].astype(BF16),
                  w_ffn_out[l].astype(BF16), _row2(final_norm_g)]
        last = l == depth - 1

        q, k2, vt, ktf, vtf, lft = _qkv_prompt_call(yp, g1, wqkv, wfl, bfl, place, rows_p)
        g0, mpart, xr_t, u_t, h_t = _mix_call(yp, mix_w, rows_p, rows_p)
        att = _flash_call(q, k2, vt)
        yp = _post_call(yp, att, g0, mpart, post_w, rows_p, last)
        outs_p[0].append(ktf)
        outs_p[1].append(vtf)
        outs_p[2].append(lft)
        outs_p[3].append(h_t[:, SUBLANES - 1])
        outs_p[4].append(xr_t[:, SUBLANES - (LRU_CONV - 1):])
        outs_p[5].append(u_t[:, SUBLANES - (SC_CONV - 1):])

        q, k, v, kb, vb, lf, fc = _qkv_sample_call(ys[0], g1, wqkv, wfl, bfl, t_dec)
        state = (_history_rows(state_lru_conv[l], t_dec), _history_rows(state_sc_conv[l], t_dec),
                 jnp.repeat(state_lru_h[l], t_dec, axis=0).reshape(1, n_tok_s, LRU_W))
        g0, mpart, xr_t, u_t, h_t = _mix_call(ys, mix_w, n_tok_s, t_dec, state)
        as_page = lambda a: jnp.pad(jnp.transpose(a.reshape(db, t_dec, -1), (0, 2, 1)),
                                    ((0, 0), (0, 0), (0, PAGE_SIZE - t_dec)))
        att = _paged_call(l, page_table, _paged_queries(q, db, t_dec), as_page(kb), as_page(vb),
                          as_page(fc), cache_kt, cache_vt, cache_lft)
        att = att.reshape(1, n_tok_s, ATT_W).astype(BF16)
        ys = _post_call(ys, att, g0, mpart, post_w, n_tok_s, last)
        outs_s[0].append(k.reshape(db, t_dec, N_HEADS, HEAD_DIM))
        outs_s[1].append(v.reshape(db, t_dec, N_HEADS, HEAD_DIM))
        outs_s[2].append(lf.reshape(db, t_dec, N_HEADS))
        outs_s[3].append(h_t.reshape(db, t_dec, LRU_W)[:, t_dec - 1])
        outs_s[4].append(xr_t.reshape(db, t_dec, LRU_W)[:, t_dec - (LRU_CONV - 1):])
        outs_s[5].append(u_t.reshape(db, t_dec, SC_W)[:, t_dec - (SC_CONV - 1):])

    kv_out = lambda o: jnp.transpose(
        jnp.stack(o).reshape(depth, bp, N_HEADS, HEAD_DIM, s_len), (0, 1, 4, 2, 3))
    return (yp, ys.reshape(db, t_dec, d),
            kv_out(outs_p[0]), kv_out(outs_p[1]), jnp.transpose(jnp.stack(outs_p[2]), (0, 1, 3, 2)),
            *[jnp.stack(o) for o in outs_p[3:]], *[jnp.stack(o) for o in outs_s])
```

```python
import functools
import math

import numpy as np
import jax
import jax.numpy as jnp
from jax import lax
from jax.experimental import pallas as pl
from jax.experimental.pallas import tpu as pltpu

N_HEADS = 8
HEAD_DIM = 64
ATT_W = N_HEADS * HEAD_DIM
LRU_W = 512
LRU_BLOCKS = 8
LRU_CONV = 4
LRU_C = 8.0
SC_W = 512
SC_CONV = 3
N_BRANCH = 3
RMS_EPS = 1e-6
NEG_INF = -1e30
PAGE_SIZE = 128
LOG2E = math.log2(math.e)

LANES = 128
SUBLANES = 8
BF16_ROWS = 16
VMEM_LIMIT = 60 * 1024 * 1024
ROW_TILE = 512
ATT_BLOCK = ROW_TILE
FFN_CHUNK = 256
PAGES_PER_STEP = 16
N_PAIRS = ATT_W // LANES
F_PARTS = 3
VT_ROWS = LANES + BF16_ROWS
PAIRS_PER_STEP = 2
KEY_SUB_BLOCK = 512
PAGED_K = ATT_W + LANES

F32 = jnp.float32
BF16 = jnp.bfloat16


def _dot(a, b):
    return jnp.dot(a, b, preferred_element_type=F32)


def _dot_nt(a, b):
    return lax.dot_general(a, b, (((1,), (1,)), ((), ())), preferred_element_type=F32)


def _rmsnorm(x, g):
    ms = jnp.mean(x * x, axis=-1, keepdims=True)
    return x * lax.rsqrt(ms + RMS_EPS) * g


def _softplus(x):
    return jnp.maximum(x, 0.0) + jnp.log1p(jnp.exp(-jnp.abs(x)))


def _bf16_pieces(x, n):
    pieces = []
    for _ in range(n - 1):
        p = x.astype(BF16)
        pieces.append(p)
        x = x - p.astype(F32)
    pieces.append(x.astype(BF16))
    return pieces


def _row_in_segment(shape, seg):
    return lax.broadcasted_iota(jnp.int32, shape, 0) & (seg - 1)


def _segment_cumsum(x, seg):
    t = _row_in_segment(x.shape, seg)
    s = 1
    while s < seg:
        x = x + jnp.where(t >= s, pltpu.roll(x, s, 0), 0.0)
        s *= 2
    return x


def _segment_linear_scan(a, b, seg):
    t = _row_in_segment(a.shape, seg)
    s = 1
    while s < seg:
        keep = t >= s
        a_prev = jnp.where(keep, pltpu.roll(a, s, 0), 1.0)
        b_prev = jnp.where(keep, pltpu.roll(b, s, 0), 0.0)
        b = a * b_prev + b
        a = a * a_prev
        s *= 2
    return a, b


def _causal_conv(u, hist, w_ref, seg):
    width = w_ref.shape[0]
    t = _row_in_segment(u.shape, seg)
    y = u * w_ref[width - 1:width, :]
    for d in range(1, width):
        shifted = jnp.where(t >= d, pltpu.roll(u, d, 0), pltpu.roll(hist, d, 0))
        y = y + shifted * w_ref[width - 1 - d:width - d, :]
    return y


def _qkv_project(x_ref, g_ref, wqkv_ref, wfl_ref, bf_ref):
    xn = _rmsnorm(x_ref[...], g_ref[...]).astype(BF16)
    z = _dot(xn, wqkv_ref[...])
    fl = _dot(xn, wfl_ref[...]) + bf_ref[...]
    lf = -_softplus(-fl)
    return z[:, :ATT_W], z[:, ATT_W:2 * ATT_W], z[:, 2 * ATT_W:], lf


def _qkv_prompt_kernel(x_ref, g_ref, wqkv_ref, wfl_ref, bf_ref, place_ref, *refs):
    q_ref, k2_ref, vt_ref, ktf_ref, vtf_ref, lft_ref, carry_ref = refs[-7:]
    rows = x_ref.shape[0]
    q, k, v, lf = _qkv_project(x_ref, g_ref, wqkv_ref, wfl_ref, bf_ref)
    q_ref[...] = (q * (HEAD_DIM ** -0.5 * LOG2E)).astype(BF16)
    ktf_ref[...] = k.T
    vt = v.T
    vtf_ref[...] = vt
    vtb = vt.astype(BF16)
    ones = jnp.ones((BF16_ROWS, rows), BF16)
    for p in range(N_PAIRS):
        vt_ref[p] = jnp.concatenate([vtb[p * LANES:(p + 1) * LANES, :], ones], axis=0)
    lft_ref[...] = lf.T[:N_HEADS, :]

    @pl.when(pl.program_id(1) == 0)
    def _():
        carry_ref[...] = jnp.zeros_like(carry_ref)

    cum = _segment_cumsum(lf, rows) + carry_ref[SUBLANES - 1:SUBLANES, :]
    carry_ref[...] = cum[rows - SUBLANES:, :]
    pieces = jnp.concatenate(_bf16_pieces(cum * LOG2E, F_PARTS), axis=1)
    placed = _dot(pieces, place_ref[...]).astype(BF16)
    kb = k.astype(BF16)
    for p in range(N_PAIRS):
        k2_ref[:, 2 * p * LANES:(2 * p + 1) * LANES] = kb[:, p * LANES:(p + 1) * LANES]
        k2_ref[:, (2 * p + 1) * LANES:(2 * p + 2) * LANES] = placed[:, p * LANES:(p + 1) * LANES]


def _qkv_prompt_call(x, g, wqkv, wfl, bfl, place, rows, layer, depth, prev):
    nb, s_len, d = x.shape
    n_t = s_len // rows
    row_spec = lambda c: pl.BlockSpec((None, rows, c), lambda b, i: (b, i, 0))
    col_spec = lambda r: pl.BlockSpec((None, None, r, rows), lambda b, i: (layer, b, 0, i))
    n_in = 6
    prev = () if prev is None else tuple(prev)
    full = lambda a: pl.BlockSpec(a.shape, lambda b, i: (0,) * a.ndim,
                                  pipeline_mode=pl.Buffered(1))
    out_shape = (
        jax.ShapeDtypeStruct((nb, s_len, ATT_W), BF16),
        jax.ShapeDtypeStruct((nb, s_len, 2 * ATT_W), BF16),
        jax.ShapeDtypeStruct((nb, N_PAIRS, n_t, VT_ROWS, rows), BF16),
        jax.ShapeDtypeStruct((depth, nb, ATT_W, s_len), F32),
        jax.ShapeDtypeStruct((depth, nb, ATT_W, s_len), F32),
        jax.ShapeDtypeStruct((depth, nb, N_HEADS, s_len), F32),
    )
    return pl.pallas_call(
        _qkv_prompt_kernel,
        grid=(nb, n_t),
        in_specs=[row_spec(d), full(g), full(wqkv), full(wfl), full(bfl), full(place)]
                 + [pl.BlockSpec(memory_space=pl.ANY)] * len(prev),
        input_output_aliases={n_in + k: 3 + k for k in range(len(prev))},
        out_specs=(row_spec(ATT_W), row_spec(2 * ATT_W),
                   pl.BlockSpec((None, N_PAIRS, None, VT_ROWS, rows), lambda b, i: (b, 0, i, 0, 0)),
                   col_spec(ATT_W), col_spec(ATT_W), col_spec(N_HEADS)),
        out_shape=out_shape,
        scratch_shapes=[pltpu.VMEM((SUBLANES, LANES), F32)],
        compiler_params=pltpu.CompilerParams(
            dimension_semantics=("arbitrary", "arbitrary"), vmem_limit_bytes=VMEM_LIMIT),
        name="qkv_prompt",
    )(x, g, wqkv, wfl, bfl, place, *prev)


def _qkv_sample_kernel(x_ref, g_ref, wqkv_ref, wfl_ref, bf_ref,
                       q_ref, k_ref, v_ref, kb_ref, vb_ref, lf_ref, fc_ref, *, seg):
    q, k, v, lf = _qkv_project(x_ref, g_ref, wqkv_ref, wfl_ref, bf_ref)
    q_ref[...] = (q * (HEAD_DIM ** -0.5)).astype(BF16)
    k_ref[...] = k
    v_ref[...] = v
    kb_ref[...] = k.astype(BF16)
    vb_ref[...] = v.astype(BF16)
    lf_ref[...] = lf[:, :N_HEADS]
    fc_ref[...] = _segment_cumsum(lf, seg)[:, :N_HEADS]


def _qkv_sample_call(x, g, wqkv, wfl, bfl, seg):
    rows, d = x.shape
    full = lambda a: pl.BlockSpec(a.shape, lambda i: (0,) * a.ndim)
    sds = lambda c, dt: jax.ShapeDtypeStruct((rows, c), dt)
    out_shape = (sds(ATT_W, BF16), sds(ATT_W, F32), sds(ATT_W, F32), sds(ATT_W, BF16),
                 sds(ATT_W, BF16), sds(N_HEADS, F32), sds(N_HEADS, F32))
    return pl.pallas_call(
        functools.partial(_qkv_sample_kernel, seg=seg),
        grid=(1,),
        in_specs=[full(x), full(g), full(wqkv), full(wfl), full(bfl)],
        out_specs=tuple(full(o) for o in out_shape),
        out_shape=out_shape,
        compiler_params=pltpu.CompilerParams(
            dimension_semantics=("arbitrary",), vmem_limit_bytes=VMEM_LIMIT),
        name="qkv_sample",
    )(x, g, wqkv, wfl, bfl)


def _mix_body(x_ref, g_ref, wrest_ref, cw_ref, cb_ref, wa_ref, ba_ref, wx_ref, bx_ref, lam_ref,
              scw_ref, wgate_ref, bgate_ref, wbl_ref, wbs_ref,
              hist_lru, hist_sc, h_in,
              g0_ref, mp_ref, xr_tail_ref, u_tail_ref, h_tail_ref, *, seg):
    x = x_ref[...]
    rows = x.shape[0]
    xn = _rmsnorm(x, g_ref[...]).astype(BF16)
    z = _dot(xn, wrest_ref[...])
    xr = z[:, :LRU_W]
    bg = z[:, LRU_W:LRU_W + SC_W]
    cg = z[:, LRU_W + SC_W:LRU_W + 2 * SC_W]
    hs = z[:, LRU_W + 2 * SC_W:]

    xc = _causal_conv(xr, hist_lru, cw_ref, seg) + cb_ref[...]
    xcb = xc.astype(BF16)
    half = LRU_W // 2
    bd = lambda w_ref: jnp.concatenate(
        [_dot(xcb[:, :half], w_ref[:half, :half]), _dot(xcb[:, half:], w_ref[half:, half:])],
        axis=1)
    r = jax.nn.sigmoid(bd(wa_ref) + ba_ref[...])
    gate_i = jax.nn.sigmoid(bd(wx_ref) + bx_ref[...])
    log_a = (-LRU_C) * r * _softplus(-lam_ref[...])
    a = jnp.exp(log_a)
    mult = jnp.sqrt(-jnp.tanh(log_a) * (a * a + 1.0))
    a_cum, b_cum = _segment_linear_scan(a, mult * gate_i * xc, seg)
    h = a_cum * h_in + b_cum

    u = cg * hs
    sc = bg * _causal_conv(u, hist_sc, scw_ref, seg)

    d = x.shape[1]
    gate = lambda j: jax.nn.sigmoid(
        _dot(xn, wgate_ref[:, j * d:(j + 1) * d]) + bgate_ref[:, j * d:(j + 1) * d])
    g0_ref[...] = gate(0).astype(BF16)
    mp_ref[...] = (gate(1) * _dot(h.astype(BF16), wbl_ref[...])
                   + gate(2) * _dot(sc.astype(BF16), wbs_ref[...]))
    tail = xr_tail_ref.shape[0]
    xr_tail_ref[...] = xr[rows - tail:, :]
    u_tail_ref[...] = u[rows - tail:, :]
    h_tail_ref[...] = h[rows - tail:, :]
    return xr, u, h


def _mix_kernel_fresh(*refs, seg):
    ins, outs, (hist_lru_ref, hist_sc_ref, h_ref) = refs[:15], refs[15:20], refs[20:]

    @pl.when(pl.program_id(1) == 0)
    def _():
        hist_lru_ref[...] = jnp.zeros_like(hist_lru_ref)
        hist_sc_ref[...] = jnp.zeros_like(hist_sc_ref)
        h_ref[...] = jnp.zeros_like(h_ref)

    xr, u, h = _mix_body(*ins, hist_lru_ref[...], hist_sc_ref[...],
                         h_ref[SUBLANES - 1:SUBLANES, :], *outs, seg=seg)
    hist_lru_ref[...] = xr
    hist_sc_ref[...] = u
    h_ref[...] = h[h.shape[0] - SUBLANES:, :]


def _mix_kernel_state(*refs, seg):
    ins, (hist_lru_ref, hist_sc_ref, h_ref), outs = refs[:15], refs[15:18], refs[18:]
    _mix_body(*ins, hist_lru_ref[...], hist_sc_ref[...], h_ref[...], *outs, seg=seg)


def _mix_call(x, weights, rows, seg, state=None):
    nb, t_len, d = x.shape
    grid = (nb, t_len // rows)
    row_spec = lambda c: pl.BlockSpec((None, rows, c), lambda b, i: (b, i, 0))
    full = lambda a: pl.BlockSpec(a.shape, lambda b, i: (0,) * a.ndim,
                                  pipeline_mode=pl.Buffered(1))
    tail = SUBLANES if state is None else rows
    n_tail = 1 if state is None else t_len // rows
    tail_spec = pl.BlockSpec((None, tail, LRU_W), lambda b, i: (b, i if state is not None else 0, 0))
    out_shape = (
        jax.ShapeDtypeStruct((nb, t_len, d), BF16),
        jax.ShapeDtypeStruct((nb, t_len, d), F32),
        jax.ShapeDtypeStruct((nb, n_tail * tail, LRU_W), F32),
        jax.ShapeDtypeStruct((nb, n_tail * tail, SC_W), F32),
        jax.ShapeDtypeStruct((nb, n_tail * tail, LRU_W), F32),
    )
    in_specs = [row_spec(d)] + [full(w) for w in weights]
    args = [x, *weights]
    if state is None:
        kern = functools.partial(_mix_kernel_fresh, seg=seg)
        scratch = [pltpu.VMEM((rows, LRU_W), F32), pltpu.VMEM((rows, SC_W), F32),
                   pltpu.VMEM((SUBLANES, LRU_W), F32)]
    else:
        kern = functools.partial(_mix_kernel_state, seg=seg)
        scratch = []
        in_specs += [row_spec(LRU_W), row_spec(SC_W), row_spec(LRU_W)]
        args += list(state)
    return pl.pallas_call(
        kern,
        grid=grid,
        in_specs=in_specs,
        out_specs=(row_spec(d), row_spec(d), tail_spec, tail_spec, tail_spec),
        out_shape=out_shape,
        scratch_shapes=scratch,
        compiler_params=pltpu.CompilerParams(
            dimension_semantics=("arbitrary", "arbitrary"), vmem_limit_bytes=VMEM_LIMIT),
        name="mix",
    )(*args)


def _flash_kernel(q_ref, k2_ref, vt_ref, sel_ref, o_ref, m_ref, acc_ref, s0_ref, s1_ref):
    blk = q_ref.shape[0]
    n_chain = vt_ref.shape[0]
    i = pl.program_id(2)
    sel = sel_ref[...]
    qabs = []
    for c in range(n_chain):
        q2 = q_ref[:, c * LANES:(c + 1) * LANES].astype(F32)
        f_lanes = lambda e: jnp.broadcast_to(sel[2 + e:3 + e, :], q2.shape)
        qabs.append(jnp.concatenate(
            [jnp.concatenate([q2 * sel[e:e + 1, :], f_lanes(e)], axis=1) for e in range(2)],
            axis=0).T.astype(BF16))

    m_ref[...] = jnp.full_like(m_ref, NEG_INF)
    acc_ref[...] = jnp.zeros_like(acc_ref)

    sub = s0_ref.shape[2]

    def score_block(j, s_ref):
        start = pl.multiple_of(j * blk, blk)
        for c in range(n_chain):
            for si in range(blk // sub):
                k2 = k2_ref[pl.ds(start + si * sub, sub), 2 * c * LANES:2 * (c + 1) * LANES]
                s_ref[c, si] = _dot(k2, qabs[c])

    def softmax_block(j, s_ref, masked):
        for c in range(n_chain):
            m_run = m_ref[c]
            for si in range(blk // sub):
                st = s_ref[c, si]
                if masked:
                    key = lax.broadcasted_iota(jnp.int32, st.shape, 0) + si * sub
                    qry = lax.broadcasted_iota(jnp.int32, st.shape, 1) & (blk - 1)
                    st = jnp.where(key <= qry, st, NEG_INF)
                m_new = jnp.maximum(m_run, jnp.max(st, axis=0, keepdims=True))
                alpha = jnp.exp2(m_run - m_new)
                p_ = jnp.exp2((st - m_new).astype(BF16))
                acc_ref[c] = alpha * acc_ref[c] + _dot(
                    vt_ref[c, j, :, si * sub:(si + 1) * sub], p_)
                m_run = m_new
            m_ref[c] = m_run

    score_block(0, s0_ref)

    def body(t, carry):
        j = 2 * t
        score_block(j + 1, s1_ref)
        softmax_block(j, s0_ref, False)
        score_block(j + 2, s0_ref)
        softmax_block(j + 1, s1_ref, False)
        return carry

    lax.fori_loop(0, i // 2, body, 0)

    @pl.when(i % 2 == 0)
    def _():
        softmax_block(i, s0_ref, True)

    @pl.when(i % 2 == 1)
    def _():
        score_block(i, s1_ref)
        softmax_block(i - 1, s0_ref, False)
        softmax_block(i, s1_ref, True)

    row = lax.broadcasted_iota(jnp.int32, (LANES, blk), 0)
    for c in range(n_chain):
        acc = acc_ref[c]
        out = acc[:LANES, :] / acc[LANES:LANES + 1, :]
        out = jnp.where(row < HEAD_DIM, out[:, :blk], out[:, blk:])
        o_ref[:, c * LANES:(c + 1) * LANES] = out.T.astype(o_ref.dtype)


def _flash_call(q, k2, vt):
    nb, s_len, _ = q.shape
    blk = vt.shape[-1]
    n_chain = PAIRS_PER_STEP
    grid = (nb, N_PAIRS // n_chain, s_len // blk)
    sub = min(KEY_SUB_BLOCK, blk)
    score_shape = (n_chain, blk // sub, sub, 2 * blk)
    sel = np.zeros((SUBLANES, LANES), np.float32)
    sel[0, :HEAD_DIM] = 1.0
    sel[1, HEAD_DIM:] = 1.0
    sel[2, :F_PARTS] = -1.0
    sel[3, F_PARTS:2 * F_PARTS] = -1.0
    sel = jnp.asarray(sel)
    return pl.pallas_call(
        _flash_kernel,
        grid=grid,
        in_specs=[
            pl.BlockSpec((None, blk, n_chain * LANES), lambda b, p, i: (b, i, p)),
            pl.BlockSpec((None, s_len, 2 * n_chain * LANES), lambda b, p, i: (b, 0, p),
                         pipeline_mode=pl.Buffered(1)),
            pl.BlockSpec((None, n_chain, s_len // blk, VT_ROWS, blk),
                         lambda b, p, i: (b, p, 0, 0, 0), pipeline_mode=pl.Buffered(1)),
            pl.BlockSpec(sel.shape, lambda b, p, i: (0, 0)),
        ],
        out_specs=pl.BlockSpec((None, blk, n_chain * LANES), lambda b, p, i: (b, i, p)),
        out_shape=jax.ShapeDtypeStruct((nb, s_len, ATT_W), BF16),
        scratch_shapes=[pltpu.VMEM((n_chain, 1, 2 * blk), F32),
                        pltpu.VMEM((n_chain, VT_ROWS, 2 * blk), F32),
                        pltpu.VMEM(score_shape, F32), pltpu.VMEM(score_shape, F32)],
        compiler_params=pltpu.CompilerParams(
            dimension_semantics=("arbitrary", "arbitrary", "arbitrary"),
            vmem_limit_bytes=VMEM_LIMIT),
        name="flash",
    )(q, k2, vt, sel)


def _paged_kernel(pt_ref, qa_ref, kn_ref, vn_ref, cn_ref, *refs, n_pages):
    del pt_ref
    k_refs = refs[:n_pages]
    v_refs = refs[n_pages:2 * n_pages]
    lf_refs = refs[2 * n_pages:3 * n_pages]
    o_ref, m_ref, l_ref, acc_ref, fc_ref = refs[3 * n_pages:]
    j = pl.program_id(1)
    n_rows = qa_ref.shape[0]
    t_len = n_rows // N_HEADS

    @pl.when(j == 0)
    def _():
        m_ref[...] = jnp.full_like(m_ref, NEG_INF)
        l_ref[...] = jnp.zeros_like(l_ref)
        acc_ref[...] = jnp.zeros_like(acc_ref)
        fc_ref[...] = jnp.zeros_like(fc_ref)

    q_bd = qa_ref[:, :ATT_W]
    q_sel = qa_ref[:, ATT_W:ATT_W + BF16_ROWS]

    def scores(k_page, cum):
        hi, lo = _bf16_pieces(cum, 2)
        pieces = jnp.concatenate([hi.astype(F32), lo.astype(F32)], axis=0).astype(BF16)
        return _dot(q_bd, k_page) + _dot(q_sel, pieces)

    def update(s_list, v_list):
        s_all = jnp.concatenate(s_list, axis=1)
        v_all = jnp.concatenate(v_list, axis=1)
        m_old = m_ref[...][:, :1]
        m_new = jnp.maximum(m_old, jnp.max(s_all, axis=1, keepdims=True))
        alpha = jnp.exp(m_old - m_new)
        p_ = jnp.exp(s_all - m_new)
        l_new = alpha * l_ref[...][:, :1] + jnp.sum(p_, axis=1, keepdims=True)
        acc_ref[...] = alpha * acc_ref[...] + _dot_nt(p_.astype(BF16), v_all)
        m_ref[...] = jnp.broadcast_to(m_new, m_ref.shape)
        l_ref[...] = jnp.broadcast_to(l_new, l_ref.shape)

    r_i = lax.broadcasted_iota(jnp.int32, (PAGE_SIZE, PAGE_SIZE), 0)
    c_i = lax.broadcasted_iota(jnp.int32, (PAGE_SIZE, PAGE_SIZE), 1)
    tri = jnp.where(r_i <= c_i, 1.0, 0.0).astype(BF16)
    lf_pages = [r[...] for r in lf_refs]
    lf_hi, lf_lo = _bf16_pieces(jnp.concatenate(lf_pages, axis=0), 2)
    cum_local = _dot(lf_hi, tri) + _dot(lf_lo, tri)

    carry = fc_ref[:, :1]
    s_list = []
    for g in range(n_pages):
        cum = cum_local[g * N_HEADS:(g + 1) * N_HEADS, :] + carry
        carry = carry + jnp.sum(lf_pages[g], axis=1, keepdims=True)
        s_list.append(scores(k_refs[g][...].astype(BF16), cum))
    fc_ref[...] = jnp.broadcast_to(carry, fc_ref.shape)
    update(s_list, [r[...].astype(BF16) for r in v_refs])

    @pl.when(j == pl.num_programs(1) - 1)
    def _():
        s = scores(kn_ref[...], carry + cn_ref[...])
        qry = lax.broadcasted_iota(jnp.int32, s.shape, 0) & (t_len - 1)
        pos = lax.broadcasted_iota(jnp.int32, s.shape, 1)
        update([jnp.where(pos <= qry, s, NEG_INF)], [vn_ref[...]])
        out = acc_ref[...] / l_ref[...][:, :1]
        for h in range(N_HEADS):
            o_ref[:, h * HEAD_DIM:(h + 1) * HEAD_DIM] = (
                out[h * t_len:(h + 1) * t_len, h * HEAD_DIM:(h + 1) * HEAD_DIM])


def _paged_call(layer, page_table, qa, k_new, v_new, cum_new, cache_kt, cache_vt, cache_lft):
    nb, n_pages = page_table.shape
    g_pages = PAGES_PER_STEP
    while n_pages % g_pages:
        g_pages //= 2
    n_rows = qa.shape[1]
    t_len = n_rows // N_HEADS
    grid = (nb, n_pages // g_pages)
    per_b = lambda shape: pl.BlockSpec((None,) + shape, lambda b, j, pt: (b,) + (0,) * len(shape))

    def page_spec(shape, g):
        return pl.BlockSpec((None, None) + shape,
                            lambda b, j, pt: (layer, pt[b, j * g_pages + g]) + (0,) * len(shape))

    kv_page = (ATT_W, PAGE_SIZE)
    in_specs = [per_b((n_rows, PAGED_K)), per_b(kv_page), per_b(kv_page),
                per_b((N_HEADS, PAGE_SIZE))]
    in_specs += [page_spec(kv_page, g) for g in range(g_pages)]
    in_specs += [page_spec(kv_page, g) for g in range(g_pages)]
    in_specs += [page_spec((N_HEADS, PAGE_SIZE), g) for g in range(g_pages)]
    grid_spec = pltpu.PrefetchScalarGridSpec(
        num_scalar_prefetch=1,
        grid=grid,
        in_specs=in_specs,
        out_specs=per_b((t_len, ATT_W)),
        scratch_shapes=[pltpu.VMEM((n_rows, LANES), F32), pltpu.VMEM((n_rows, LANES), F32),
                        pltpu.VMEM((n_rows, ATT_W), F32), pltpu.VMEM((N_HEADS, LANES), F32)],
    )
    return pl.pallas_call(
        functools.partial(_paged_kernel, n_pages=g_pages),
        grid_spec=grid_spec,
        out_shape=jax.ShapeDtypeStruct((nb, t_len, ATT_W), F32),
        compiler_params=pltpu.CompilerParams(
            dimension_semantics=("arbitrary", "arbitrary"), vmem_limit_bytes=VMEM_LIMIT),
        name="paged",
    )(page_table, qa, k_new, v_new, cum_new,
      *([cache_kt] * g_pages), *([cache_vt] * g_pages), *([cache_lft] * g_pages))


def _post_kernel(x_ref, att_ref, g0_ref, mp_ref, wba_ref, wout_ref, g2_ref, wg_ref, wu_ref,
                 wdown_ref, gf_ref, y_ref, *, final_norm):
    merged = g0_ref[...].astype(F32) * _dot(att_ref[...], wba_ref[...]) + mp_ref[...]
    h = x_ref[...] + _dot(merged.astype(BF16), wout_ref[...])
    hn = _rmsnorm(h, g2_ref[...]).astype(BF16)
    d_ff = wg_ref.shape[1]
    y = h
    for c in range(0, d_ff, FFN_CHUNK):
        g = _dot(hn, wg_ref[:, c:c + FFN_CHUNK])
        u = _dot(hn, wu_ref[:, c:c + FFN_CHUNK])
        act = (g * jax.nn.sigmoid(g) * u).astype(BF16)
        y = y + _dot(act, wdown_ref[c:c + FFN_CHUNK, :])
    if final_norm:
        y = _rmsnorm(y, gf_ref[...])
    y_ref[...] = y


def _post_call(x, att, g0, mpart, weights, rows, final_norm):
    nb, t_len, d = x.shape
    grid = (nb, t_len // rows)
    row_spec = lambda c: pl.BlockSpec((None, rows, c), lambda b, i: (b, i, 0))
    full = lambda a: pl.BlockSpec(a.shape, lambda b, i: (0,) * a.ndim,
                                  pipeline_mode=pl.Buffered(1))
    return pl.pallas_call(
        functools.partial(_post_kernel, final_norm=final_norm),
        grid=grid,
        in_specs=[row_spec(d), row_spec(ATT_W), row_spec(d), row_spec(d)]
                 + [full(w) for w in weights],
        out_specs=row_spec(d),
        out_shape=jax.ShapeDtypeStruct((nb, t_len, d), F32),
        compiler_params=pltpu.CompilerParams(
            dimension_semantics=("arbitrary", "arbitrary"), vmem_limit_bytes=VMEM_LIMIT),
        name="post",
    )(x, att, g0, mpart, *weights)


def _block_diag(w):
    n, d, e = w.shape
    eye = jnp.eye(n, dtype=w.dtype)
    return (w[:, :, None, :] * eye[:, None, :, None]).reshape(n * d, n * e)


def _row2(v):
    return v.reshape(1, -1).astype(F32)


def _history_rows(buf, t_len):
    nb, w1, c = buf.shape
    padded = jnp.concatenate([jnp.zeros((nb, t_len - w1, c), buf.dtype), buf], axis=1)
    return jnp.roll(padded, -1, axis=0).reshape(1, nb * t_len, c)


def _f_placement():
    place = np.zeros((F_PARTS * LANES, N_PAIRS * LANES), np.float32)
    for k in range(F_PARTS):
        for h in range(N_HEADS):
            place[k * LANES + h, (h // 2) * LANES + F_PARTS * (h % 2) + k] = 1.0
    return jnp.asarray(place, BF16)


def _paged_queries(q, db, t_dec):
    q4 = jnp.transpose(q.reshape(db, t_dec, N_HEADS, HEAD_DIM), (0, 2, 1, 3))
    eye = jnp.eye(N_HEADS, dtype=q.dtype)
    q_bd = (q4[:, :, :, None, :] * eye[None, :, None, :, None]).reshape(
        db, N_HEADS * t_dec, ATT_W)
    sel = -jnp.repeat(eye, t_dec, axis=0)
    aug = jnp.concatenate(
        [sel, sel, jnp.zeros((N_HEADS * t_dec, PAGED_K - ATT_W - 2 * N_HEADS), q.dtype)], axis=1)
    return jnp.concatenate([q_bd, jnp.broadcast_to(aug, (db,) + aug.shape)], axis=2)


def kernel(x_prompt, x_sample, cache_k, cache_v, cache_logf, state_lru_h, state_lru_conv,
           state_sc_conv, page_table, norm1_g, w_in, b_f, lru_conv_w, lru_conv_b, lru_w_a,
           lru_b_a, lru_w_x, lru_b_x, lru_lambda, sc_conv_w, w_gate, b_gate, w_br_att,
           w_br_lru, w_br_sc, w_out, norm2_g, w_ffn_in, w_ffn_out, final_norm_g):
    depth = w_in.shape[0]
    bp, s_len, d = x_prompt.shape
    db, t_dec, _ = x_sample.shape
    n_tok_s = db * t_dec
    d_ff = w_ffn_out.shape[1]
    rows_p = min(ROW_TILE, s_len)
    n_pool = cache_k.shape[1]

    cache_kt = jnp.transpose(cache_k, (0, 1, 3, 4, 2)).reshape(depth, n_pool, ATT_W, PAGE_SIZE)
    cache_vt = jnp.transpose(cache_v, (0, 1, 3, 4, 2)).reshape(depth, n_pool, ATT_W, PAGE_SIZE)
    cache_lft = jnp.transpose(cache_logf, (0, 1, 3, 2))
    place = _f_placement()

    yp = x_prompt
    ys = x_sample.reshape(1, n_tok_s, d)
    outs_p = [[] for _ in range(3)]
    outs_s = [[] for _ in range(6)]
    kv_all = None
    q_off = 3 * ATT_W
    r_off = q_off + N_HEADS
    for l in range(depth):
        wi = w_in[l]
        wqkv = wi[:, :q_off].astype(BF16)
        wfl = jnp.pad(wi[:, q_off:r_off], ((0, 0), (0, LANES - N_HEADS))).astype(BF16)
        bfl = jnp.pad(b_f[l].reshape(1, -1), ((0, 0), (0, LANES - N_HEADS))).astype(F32)
        g1 = _row2(norm1_g[l])
        mix_w = [g1, wi[:, r_off:].astype(BF16), lru_conv_w[l], _row2(lru_conv_b[l]),
                 _block_diag(lru_w_a[l]).astype(BF16), _row2(lru_b_a[l]),
                 _block_diag(lru_w_x[l]).astype(BF16), _row2(lru_b_x[l]), _row2(lru_lambda[l]),
                 sc_conv_w[l], w_gate[l].astype(BF16), _row2(b_gate[l]),
                 w_br_lru[l].astype(BF16), w_br_sc[l].astype(BF16)]
        post_w = [w_br_att[l].astype(BF16), w_out[l].astype(BF16), _row2(norm2_g[l]),
                  w_ffn_in[l][:, :d_ff].astype(BF16), w_ffn_in[l][:, d_ff:].astype(BF16),
                  w_ffn_out[l].astype(BF16), _row2(final_norm_g)]
        last = l == depth - 1

        q, k2, vt, *kv_all = _qkv_prompt_call(yp, g1, wqkv, wfl, bfl, place, rows_p, l, depth,
                                              kv_all)
        g0, mpart, xr_t, u_t, h_t = _mix_call(yp, mix_w, rows_p, rows_p)
        att = _flash_call(q, k2, vt)
        yp = _post_call(yp, att, g0, mpart, post_w, rows_p, last)
        outs_p[0].append(h_t[:, SUBLANES - 1])
        outs_p[1].append(xr_t[:, SUBLANES - (LRU_CONV - 1):])
        outs_p[2].append(u_t[:, SUBLANES - (SC_CONV - 1):])

        q, k, v, kb, vb, lf, fc = _qkv_sample_call(ys[0], g1, wqkv, wfl, bfl, t_dec)
        state = (_history_rows(state_lru_conv[l], t_dec), _history_rows(state_sc_conv[l], t_dec),
                 jnp.repeat(state_lru_h[l], t_dec, axis=0).reshape(1, n_tok_s, LRU_W))
        g0, mpart, xr_t, u_t, h_t = _mix_call(ys, mix_w, n_tok_s, t_dec, state)
        as_page = lambda a: jnp.pad(jnp.transpose(a.reshape(db, t_dec, -1), (0, 2, 1)),
                                    ((0, 0), (0, 0), (0, PAGE_SIZE - t_dec)))
        att = _paged_call(l, page_table, _paged_queries(q, db, t_dec), as_page(kb), as_page(vb),
                          as_page(fc), cache_kt, cache_vt, cache_lft)
        att = att.reshape(1, n_tok_s, ATT_W).astype(BF16)
        ys = _post_call(ys, att, g0, mpart, post_w, n_tok_s, last)
        outs_s[0].append(k.reshape(db, t_dec, N_HEADS, HEAD_DIM))
        outs_s[1].append(v.reshape(db, t_dec, N_HEADS, HEAD_DIM))
        outs_s[2].append(lf.reshape(db, t_dec, N_HEADS))
        outs_s[3].append(h_t.reshape(db, t_dec, LRU_W)[:, t_dec - 1])
        outs_s[4].append(xr_t.reshape(db, t_dec, LRU_W)[:, t_dec - (LRU_CONV - 1):])
        outs_s[5].append(u_t.reshape(db, t_dec, SC_W)[:, t_dec - (SC_CONV - 1):])

    kv_out = lambda o: jnp.transpose(
        o.reshape(depth, bp, N_HEADS, HEAD_DIM, s_len), (0, 1, 4, 2, 3))
    return (yp, ys.reshape(db, t_dec, d),
            kv_out(kv_all[0]), kv_out(kv_all[1]), jnp.transpose(kv_all[2], (0, 1, 3, 2)),
            *[jnp.stack(o) for o in outs_p], *[jnp.stack(o) for o in outs_s])
```

```python
import functools
import math

import numpy as np
import jax
import jax.numpy as jnp
from jax import lax
from jax.experimental import pallas as pl
from jax.experimental.pallas import tpu as pltpu

N_HEADS = 8
HEAD_DIM = 64
ATT_W = N_HEADS * HEAD_DIM
LRU_W = 512
LRU_BLOCKS = 8
LRU_CONV = 4
LRU_C = 8.0
SC_W = 512
SC_CONV = 3
N_BRANCH = 3
RMS_EPS = 1e-6
NEG_INF = -1e30
PAGE_SIZE = 128
LOG2E = math.log2(math.e)

LANES = 128
SUBLANES = 8
BF16_ROWS = 16
VMEM_LIMIT = 60 * 1024 * 1024
ROW_TILE = 512
ATT_BLOCK = ROW_TILE
FFN_CHUNK = 256
PAGES_PER_STEP = 16
N_PAIRS = ATT_W // LANES
F_PARTS = 3
VT_ROWS = LANES + BF16_ROWS
PAIRS_PER_STEP = 2
KEY_SUB_BLOCK = 512
PAGED_K = ATT_W + LANES

F32 = jnp.float32
BF16 = jnp.bfloat16


def _dot(a, b):
    return jnp.dot(a, b, preferred_element_type=F32)


def _dot_nt(a, b):
    return lax.dot_general(a, b, (((1,), (1,)), ((), ())), preferred_element_type=F32)


def _rmsnorm(x, g):
    ms = jnp.mean(x * x, axis=-1, keepdims=True)
    return x * lax.rsqrt(ms + RMS_EPS) * g


def _softplus(x):
    return jnp.maximum(x, 0.0) + jnp.log1p(jnp.exp(-jnp.abs(x)))


def _bf16_pieces(x, n):
    pieces = []
    for _ in range(n - 1):
        p = x.astype(BF16)
        pieces.append(p)
        x = x - p.astype(F32)
    pieces.append(x.astype(BF16))
    return pieces


def _row_in_segment(shape, seg):
    return lax.broadcasted_iota(jnp.int32, shape, 0) & (seg - 1)


def _segment_cumsum(x, seg):
    t = _row_in_segment(x.shape, seg)
    s = 1
    while s < seg:
        x = x + jnp.where(t >= s, pltpu.roll(x, s, 0), 0.0)
        s *= 2
    return x


def _segment_linear_scan(a, b, seg):
    t = _row_in_segment(a.shape, seg)
    s = 1
    while s < seg:
        keep = t >= s
        a_prev = jnp.where(keep, pltpu.roll(a, s, 0), 1.0)
        b_prev = jnp.where(keep, pltpu.roll(b, s, 0), 0.0)
        b = a * b_prev + b
        a = a * a_prev
        s *= 2
    return a, b


def _causal_conv(u, hist, w_ref, seg):
    width = w_ref.shape[0]
    rows = u.shape[0]
    y = u * w_ref[width - 1:width, :]
    t = _row_in_segment(hist.shape, SUBLANES if seg == rows else seg)
    for d in range(1, width):
        rolled = pltpu.roll(u, d, 0)
        if seg == rows:
            head = jnp.where(t >= d, rolled[:SUBLANES, :], pltpu.roll(hist, d, 0))
            shifted = jnp.concatenate([head, rolled[SUBLANES:, :]], axis=0)
        else:
            shifted = jnp.where(t >= d, rolled, pltpu.roll(hist, d, 0))
        y = y + shifted * w_ref[width - 1 - d:width - d, :]
    return y


def _linear_scan(a, b, h_in, seg):
    rows = a.shape[0]
    if seg != rows or rows == SUBLANES:
        a_cum, b_cum = _segment_linear_scan(a, b, seg)
        return a_cum * h_in + b_cum
    a_cum, b_cum = _segment_linear_scan(a, b, SUBLANES)
    carry = h_in
    groups = []
    for r0 in range(0, rows, SUBLANES):
        hg = a_cum[r0:r0 + SUBLANES, :] * carry + b_cum[r0:r0 + SUBLANES, :]
        groups.append(hg)
        carry = hg[SUBLANES - 1:, :]
    return jnp.concatenate(groups, axis=0)


def _qkv_project(x_ref, g_ref, wqkv_ref, wfl_ref, bf_ref):
    xn = _rmsnorm(x_ref[...], g_ref[...]).astype(BF16)
    z = _dot(xn, wqkv_ref[...])
    fl = _dot(xn, wfl_ref[...]) + bf_ref[...]
    lf = -_softplus(-fl)
    return z[:, :ATT_W], z[:, ATT_W:2 * ATT_W], z[:, 2 * ATT_W:], lf


def _qkv_prompt_kernel(x_ref, g_ref, wqkv_ref, wfl_ref, bf_ref, place_ref, *refs):
    q_ref, k2_ref, vt_ref, ktf_ref, vtf_ref, lft_ref, carry_ref = refs[-7:]
    rows = x_ref.shape[0]
    q, k, v, lf = _qkv_project(x_ref, g_ref, wqkv_ref, wfl_ref, bf_ref)
    q_ref[...] = (q * (HEAD_DIM ** -0.5 * LOG2E)).astype(BF16)
    ktf_ref[...] = k.T
    vt = v.T
    vtf_ref[...] = vt
    vtb = vt.astype(BF16)
    ones = jnp.ones((BF16_ROWS, rows), BF16)
    for p in range(N_PAIRS):
        vt_ref[p] = jnp.concatenate([vtb[p * LANES:(p + 1) * LANES, :], ones], axis=0)
    lft_ref[...] = lf.T[:N_HEADS, :]

    @pl.when(pl.program_id(1) == 0)
    def _():
        carry_ref[...] = jnp.zeros_like(carry_ref)

    cum = _segment_cumsum(lf, rows) + carry_ref[SUBLANES - 1:SUBLANES, :]
    carry_ref[...] = cum[rows - SUBLANES:, :]
    pieces = jnp.concatenate(_bf16_pieces(cum * LOG2E, F_PARTS), axis=1)
    placed = _dot(pieces, place_ref[...]).astype(BF16)
    kb = k.astype(BF16)
    for p in range(N_PAIRS):
        k2_ref[:, 2 * p * LANES:(2 * p + 1) * LANES] = kb[:, p * LANES:(p + 1) * LANES]
        k2_ref[:, (2 * p + 1) * LANES:(2 * p + 2) * LANES] = placed[:, p * LANES:(p + 1) * LANES]


def _qkv_prompt_call(x, g, wqkv, wfl, bfl, place, rows, layer, depth, prev):
    nb, s_len, d = x.shape
    n_t = s_len // rows
    row_spec = lambda c: pl.BlockSpec((None, rows, c), lambda b, i: (b, i, 0))
    col_spec = lambda r: pl.BlockSpec((None, None, r, rows), lambda b, i: (layer, b, 0, i))
    n_in = 6
    prev = () if prev is None else tuple(prev)
    full = lambda a: pl.BlockSpec(a.shape, lambda b, i: (0,) * a.ndim,
                                  pipeline_mode=pl.Buffered(1))
    out_shape = (
        jax.ShapeDtypeStruct((nb, s_len, ATT_W), BF16),
        jax.ShapeDtypeStruct((nb, s_len, 2 * ATT_W), BF16),
        jax.ShapeDtypeStruct((nb, N_PAIRS, n_t, VT_ROWS, rows), BF16),
        jax.ShapeDtypeStruct((depth, nb, ATT_W, s_len), F32),
        jax.ShapeDtypeStruct((depth, nb, ATT_W, s_len), F32),
        jax.ShapeDtypeStruct((depth, nb, N_HEADS, s_len), F32),
    )
    return pl.pallas_call(
        _qkv_prompt_kernel,
        grid=(nb, n_t),
        in_specs=[row_spec(d), full(g), full(wqkv), full(wfl), full(bfl), full(place)]
                 + [pl.BlockSpec(memory_space=pl.ANY)] * len(prev),
        input_output_aliases={n_in + k: 3 + k for k in range(len(prev))},
        out_specs=(row_spec(ATT_W), row_spec(2 * ATT_W),
                   pl.BlockSpec((None, N_PAIRS, None, VT_ROWS, rows), lambda b, i: (b, 0, i, 0, 0)),
                   col_spec(ATT_W), col_spec(ATT_W), col_spec(N_HEADS)),
        out_shape=out_shape,
        scratch_shapes=[pltpu.VMEM((SUBLANES, LANES), F32)],
        compiler_params=pltpu.CompilerParams(
            dimension_semantics=("arbitrary", "arbitrary"), vmem_limit_bytes=VMEM_LIMIT),
        name="qkv_prompt",
    )(x, g, wqkv, wfl, bfl, place, *prev)


def _qkv_sample_kernel(x_ref, g_ref, wqkv_ref, wfl_ref, bf_ref,
                       q_ref, k_ref, v_ref, kb_ref, vb_ref, lf_ref, fc_ref, *, seg):
    q, k, v, lf = _qkv_project(x_ref, g_ref, wqkv_ref, wfl_ref, bf_ref)
    q_ref[...] = (q * (HEAD_DIM ** -0.5)).astype(BF16)
    k_ref[...] = k
    v_ref[...] = v
    kb_ref[...] = k.astype(BF16)
    vb_ref[...] = v.astype(BF16)
    lf_ref[...] = lf[:, :N_HEADS]
    fc_ref[...] = _segment_cumsum(lf, seg)[:, :N_HEADS]


def _qkv_sample_call(x, g, wqkv, wfl, bfl, seg):
    rows, d = x.shape
    full = lambda a: pl.BlockSpec(a.shape, lambda i: (0,) * a.ndim)
    sds = lambda c, dt: jax.ShapeDtypeStruct((rows, c), dt)
    out_shape = (sds(ATT_W, BF16), sds(ATT_W, F32), sds(ATT_W, F32), sds(ATT_W, BF16),
                 sds(ATT_W, BF16), sds(N_HEADS, F32), sds(N_HEADS, F32))
    return pl.pallas_call(
        functools.partial(_qkv_sample_kernel, seg=seg),
        grid=(1,),
        in_specs=[full(x), full(g), full(wqkv), full(wfl), full(bfl)],
        out_specs=tuple(full(o) for o in out_shape),
        out_shape=out_shape,
        compiler_params=pltpu.CompilerParams(
            dimension_semantics=("arbitrary",), vmem_limit_bytes=VMEM_LIMIT),
        name="qkv_sample",
    )(x, g, wqkv, wfl, bfl)


def _mix_body(x_ref, g_ref, wrest_ref, cw_ref, cb_ref, wa_ref, ba_ref, wx_ref, bx_ref, lam_ref,
              scw_ref, wgate_ref, bgate_ref, wbl_ref, wbs_ref,
              hist_lru, hist_sc, h_in,
              g0_ref, mp_ref, xr_tail_ref, u_tail_ref, h_tail_ref, *, seg):
    x = x_ref[...]
    rows = x.shape[0]
    xn = _rmsnorm(x, g_ref[...]).astype(BF16)
    z = _dot(xn, wrest_ref[...])
    xr = z[:, :LRU_W]
    bg = z[:, LRU_W:LRU_W + SC_W]
    cg = z[:, LRU_W + SC_W:LRU_W + 2 * SC_W]
    hs = z[:, LRU_W + 2 * SC_W:]

    xc = _causal_conv(xr, hist_lru, cw_ref, seg) + cb_ref[...]
    xcb = xc.astype(BF16)
    half = LRU_W // 2
    bd = lambda w_ref: jnp.concatenate(
        [_dot(xcb[:, :half], w_ref[:half, :half]), _dot(xcb[:, half:], w_ref[half:, half:])],
        axis=1)
    r = jax.nn.sigmoid(bd(wa_ref) + ba_ref[...])
    gate_i = jax.nn.sigmoid(bd(wx_ref) + bx_ref[...])
    log_a = (-LRU_C) * r * _softplus(-lam_ref[...])
    a = jnp.exp(log_a)
    mult = jnp.sqrt(-jnp.tanh(log_a) * (a * a + 1.0))
    h = _linear_scan(a, mult * gate_i * xc, h_in, seg)

    u = cg * hs
    sc = bg * _causal_conv(u, hist_sc, scw_ref, seg)

    d = x.shape[1]
    gate = lambda j: jax.nn.sigmoid(
        _dot(xn, wgate_ref[:, j * d:(j + 1) * d]) + bgate_ref[:, j * d:(j + 1) * d])
    g0_ref[...] = gate(0).astype(BF16)
    mp_ref[...] = (gate(1) * _dot(h.astype(BF16), wbl_ref[...])
                   + gate(2) * _dot(sc.astype(BF16), wbs_ref[...]))
    tail = xr_tail_ref.shape[0]
    xr_tail_ref[...] = xr[rows - tail:, :]
    u_tail_ref[...] = u[rows - tail:, :]
    h_tail_ref[...] = h[rows - tail:, :]
    return xr, u, h


def _mix_kernel_fresh(*refs, seg):
    ins, outs, (hist_lru_ref, hist_sc_ref, h_ref) = refs[:15], refs[15:20], refs[20:]

    @pl.when(pl.program_id(1) == 0)
    def _():
        hist_lru_ref[...] = jnp.zeros_like(hist_lru_ref)
        hist_sc_ref[...] = jnp.zeros_like(hist_sc_ref)
        h_ref[...] = jnp.zeros_like(h_ref)

    xr, u, h = _mix_body(*ins, hist_lru_ref[...], hist_sc_ref[...],
                         h_ref[SUBLANES - 1:SUBLANES, :], *outs, seg=seg)
    tail = h.shape[0] - SUBLANES
    hist_lru_ref[...] = xr[tail:, :]
    hist_sc_ref[...] = u[tail:, :]
    h_ref[...] = h[tail:, :]


def _mix_kernel_state(*refs, seg):
    ins, (hist_lru_ref, hist_sc_ref, h_ref), outs = refs[:15], refs[15:18], refs[18:]
    _mix_body(*ins, hist_lru_ref[...], hist_sc_ref[...], h_ref[...], *outs, seg=seg)


def _mix_call(x, weights, rows, seg, state=None):
    nb, t_len, d = x.shape
    grid = (nb, t_len // rows)
    row_spec = lambda c: pl.BlockSpec((None, rows, c), lambda b, i: (b, i, 0))
    full = lambda a: pl.BlockSpec(a.shape, lambda b, i: (0,) * a.ndim,
                                  pipeline_mode=pl.Buffered(1))
    tail = SUBLANES if state is None else rows
    n_tail = 1 if state is None else t_len // rows
    tail_spec = pl.BlockSpec((None, tail, LRU_W), lambda b, i: (b, i if state is not None else 0, 0))
    out_shape = (
        jax.ShapeDtypeStruct((nb, t_len, d), BF16),
        jax.ShapeDtypeStruct((nb, t_len, d), F32),
        jax.ShapeDtypeStruct((nb, n_tail * tail, LRU_W), F32),
        jax.ShapeDtypeStruct((nb, n_tail * tail, SC_W), F32),
        jax.ShapeDtypeStruct((nb, n_tail * tail, LRU_W), F32),
    )
    in_specs = [row_spec(d)] + [full(w) for w in weights]
    args = [x, *weights]
    if state is None:
        kern = functools.partial(_mix_kernel_fresh, seg=seg)
        scratch = [pltpu.VMEM((SUBLANES, LRU_W), F32), pltpu.VMEM((SUBLANES, SC_W), F32),
                   pltpu.VMEM((SUBLANES, LRU_W), F32)]
    else:
        kern = functools.partial(_mix_kernel_state, seg=seg)
        scratch = []
        in_specs += [row_spec(LRU_W), row_spec(SC_W), row_spec(LRU_W)]
        args += list(state)
    return pl.pallas_call(
        kern,
        grid=grid,
        in_specs=in_specs,
        out_specs=(row_spec(d), row_spec(d), tail_spec, tail_spec, tail_spec),
        out_shape=out_shape,
        scratch_shapes=scratch,
        compiler_params=pltpu.CompilerParams(
            dimension_semantics=("arbitrary", "arbitrary"), vmem_limit_bytes=VMEM_LIMIT),
        name="mix",
    )(*args)


def _flash_kernel(q_ref, k2_ref, vt_ref, sel_ref, o_ref, m_ref, acc_ref, s0_ref, s1_ref):
    blk = q_ref.shape[0]
    n_chain = vt_ref.shape[0]
    i = pl.program_id(2)
    sel = sel_ref[...]
    qabs = []
    for c in range(n_chain):
        q2 = q_ref[:, c * LANES:(c + 1) * LANES].astype(F32)
        f_lanes = lambda e: jnp.broadcast_to(sel[2 + e:3 + e, :], q2.shape)
        qabs.append(jnp.concatenate(
            [jnp.concatenate([q2 * sel[e:e + 1, :], f_lanes(e)], axis=1) for e in range(2)],
            axis=0).T.astype(BF16))

    m_ref[...] = jnp.full_like(m_ref, NEG_INF)
    acc_ref[...] = jnp.zeros_like(acc_ref)

    sub = s0_ref.shape[2]

    def score_block(j, s_ref):
        start = pl.multiple_of(j * blk, blk)
        for c in range(n_chain):
            for si in range(blk // sub):
                k2 = k2_ref[pl.ds(start + si * sub, sub), 2 * c * LANES:2 * (c + 1) * LANES]
                s_ref[c, si] = _dot(k2, qabs[c])

    def softmax_block(j, s_ref, masked):
        for c in range(n_chain):
            m_run = m_ref[c]
            for si in range(blk // sub):
                st = s_ref[c, si]
                if masked:
                    key = lax.broadcasted_iota(jnp.int32, st.shape, 0) + si * sub
                    qry = lax.broadcasted_iota(jnp.int32, st.shape, 1) & (blk - 1)
                    st = jnp.where(key <= qry, st, NEG_INF)
                m_new = jnp.maximum(m_run, jnp.max(st, axis=0, keepdims=True))
                alpha = jnp.exp2(m_run - m_new)
                p_ = jnp.exp2((st - m_new).astype(BF16))
                acc_ref[c] = alpha * acc_ref[c] + _dot(
                    vt_ref[c, j, :, si * sub:(si + 1) * sub], p_)
                m_run = m_new
            m_ref[c] = m_run

    score_block(0, s0_ref)

    def body(t, carry):
        j = 2 * t
        score_block(j + 1, s1_ref)
        softmax_block(j, s0_ref, False)
        score_block(j + 2, s0_ref)
        softmax_block(j + 1, s1_ref, False)
        return carry

    lax.fori_loop(0, i // 2, body, 0)

    @pl.when(i % 2 == 0)
    def _():
        softmax_block(i, s0_ref, True)

    @pl.when(i % 2 == 1)
    def _():
        score_block(i, s1_ref)
        softmax_block(i - 1, s0_ref, False)
        softmax_block(i, s1_ref, True)

    row = lax.broadcasted_iota(jnp.int32, (LANES, blk), 0)
    for c in range(n_chain):
        acc = acc_ref[c]
        out = acc[:LANES, :] / acc[LANES:LANES + 1, :]
        out = jnp.where(row < HEAD_DIM, out[:, :blk], out[:, blk:])
        o_ref[:, c * LANES:(c + 1) * LANES] = out.T.astype(o_ref.dtype)


def _flash_call(q, k2, vt):
    nb, s_len, _ = q.shape
    blk = vt.shape[-1]
    n_chain = PAIRS_PER_STEP
    grid = (nb, N_PAIRS // n_chain, s_len // blk)
    sub = min(KEY_SUB_BLOCK, blk)
    score_shape = (n_chain, blk // sub, sub, 2 * blk)
    sel = np.zeros((SUBLANES, LANES), np.float32)
    sel[0, :HEAD_DIM] = 1.0
    sel[1, HEAD_DIM:] = 1.0
    sel[2, :F_PARTS] = -1.0
    sel[3, F_PARTS:2 * F_PARTS] = -1.0
    sel = jnp.asarray(sel)
    return pl.pallas_call(
        _flash_kernel,
        grid=grid,
        in_specs=[
            pl.BlockSpec((None, blk, n_chain * LANES), lambda b, p, i: (b, i, p)),
            pl.BlockSpec((None, s_len, 2 * n_chain * LANES), lambda b, p, i: (b, 0, p),
                         pipeline_mode=pl.Buffered(1)),
            pl.BlockSpec((None, n_chain, s_len // blk, VT_ROWS, blk),
                         lambda b, p, i: (b, p, 0, 0, 0), pipeline_mode=pl.Buffered(1)),
            pl.BlockSpec(sel.shape, lambda b, p, i: (0, 0)),
        ],
        out_specs=pl.BlockSpec((None, blk, n_chain * LANES), lambda b, p, i: (b, i, p)),
        out_shape=jax.ShapeDtypeStruct((nb, s_len, ATT_W), BF16),
        scratch_shapes=[pltpu.VMEM((n_chain, 1, 2 * blk), F32),
                        pltpu.VMEM((n_chain, VT_ROWS, 2 * blk), F32),
                        pltpu.VMEM(score_shape, F32), pltpu.VMEM(score_shape, F32)],
        compiler_params=pltpu.CompilerParams(
            dimension_semantics=("arbitrary", "arbitrary", "arbitrary"),
            vmem_limit_bytes=VMEM_LIMIT),
        name="flash",
    )(q, k2, vt, sel)


def _paged_kernel(pt_ref, qa_ref, kn_ref, vn_ref, cn_ref, ck_ref, cv_ref, clf_ref,
                  o_ref, m_ref, l_ref, acc_ref, fc_ref, kbuf, vbuf, lfbuf, sem, *, layer):
    n_pages = kbuf.shape[1]
    b = pl.program_id(0)
    j = pl.program_id(1)
    n_j = pl.num_programs(1)
    step = b * n_j + j
    slot = lax.rem(step, 2)
    n_rows = qa_ref.shape[0]
    t_len = n_rows // N_HEADS

    def page_copies(bb, jj, slot_):
        copies = []
        for g in range(n_pages):
            page = pt_ref[bb, jj * n_pages + g]
            for src, dst in ((ck_ref, kbuf), (cv_ref, vbuf), (clf_ref, lfbuf)):
                copies.append(pltpu.make_async_copy(
                    src.at[layer, page], dst.at[slot_, g], sem.at[slot_]))
        return copies

    @pl.when(step == 0)
    def _():
        for c in page_copies(b, j, slot):
            c.start()

    @pl.when(step + 1 < pl.num_programs(0) * n_j)
    def _():
        wrap = j + 1 == n_j
        for c in page_copies(jnp.where(wrap, b + 1, b), jnp.where(wrap, 0, j + 1), 1 - slot):
            c.start()

    for c in page_copies(b, j, slot):
        c.wait()
    k_refs = [kbuf.at[slot, g] for g in range(n_pages)]
    v_refs = [vbuf.at[slot, g] for g in range(n_pages)]
    lf_refs = [lfbuf.at[slot, g] for g in range(n_pages)]

    @pl.when(j == 0)
    def _():
        m_ref[...] = jnp.full_like(m_ref, NEG_INF)
        l_ref[...] = jnp.zeros_like(l_ref)
        acc_ref[...] = jnp.zeros_like(acc_ref)
        fc_ref[...] = jnp.zeros_like(fc_ref)

    q_bd = qa_ref[:, :ATT_W]
    q_sel = qa_ref[:, ATT_W:ATT_W + BF16_ROWS]

    def scores(k_page, cum):
        hi, lo = _bf16_pieces(cum, 2)
        pieces = jnp.concatenate([hi.astype(F32), lo.astype(F32)], axis=0).astype(BF16)
        return _dot(q_bd, k_page) + _dot(q_sel, pieces)

    def update(s_list, v_list):
        s_all = jnp.concatenate(s_list, axis=1)
        v_all = jnp.concatenate(v_list, axis=1)
        m_old = m_ref[...][:, :1]
        m_new = jnp.maximum(m_old, jnp.max(s_all, axis=1, keepdims=True))
        alpha = jnp.exp(m_old - m_new)
        p_ = jnp.exp(s_all - m_new)
        l_new = alpha * l_ref[...][:, :1] + jnp.sum(p_, axis=1, keepdims=True)
        acc_ref[...] = alpha * acc_ref[...] + _dot_nt(p_.astype(BF16), v_all)
        m_ref[...] = jnp.broadcast_to(m_new, m_ref.shape)
        l_ref[...] = jnp.broadcast_to(l_new, l_ref.shape)

    r_i = lax.broadcasted_iota(jnp.int32, (PAGE_SIZE, PAGE_SIZE), 0)
    c_i = lax.broadcasted_iota(jnp.int32, (PAGE_SIZE, PAGE_SIZE), 1)
    tri = jnp.where(r_i <= c_i, 1.0, 0.0).astype(BF16)
    lf_pages = [r[...] for r in lf_refs]
    lf_hi, lf_lo = _bf16_pieces(jnp.concatenate(lf_pages, axis=0), 2)
    cum_local = _dot(lf_hi, tri) + _dot(lf_lo, tri)

    carry = fc_ref[:, :1]
    s_list = []
    for g in range(n_pages):
        cum = cum_local[g * N_HEADS:(g + 1) * N_HEADS, :] + carry
        carry = carry + jnp.sum(lf_pages[g], axis=1, keepdims=True)
        s_list.append(scores(k_refs[g][...].astype(BF16), cum))
    fc_ref[...] = jnp.broadcast_to(carry, fc_ref.shape)
    update(s_list, [r[...].astype(BF16) for r in v_refs])

    @pl.when(j == pl.num_programs(1) - 1)
    def _():
        s = scores(kn_ref[...], carry + cn_ref[...])
        qry = lax.broadcasted_iota(jnp.int32, s.shape, 0) & (t_len - 1)
        pos = lax.broadcasted_iota(jnp.int32, s.shape, 1)
        update([jnp.where(pos <= qry, s, NEG_INF)], [vn_ref[...]])
        out = acc_ref[...] / l_ref[...][:, :1]
        for h in range(N_HEADS):
            o_ref[:, h * HEAD_DIM:(h + 1) * HEAD_DIM] = (
                out[h * t_len:(h + 1) * t_len, h * HEAD_DIM:(h + 1) * HEAD_DIM])


def _paged_call(layer, page_table, qa, k_new, v_new, cum_new, cache_kt, cache_vt, cache_lft):
    nb, n_pages = page_table.shape
    g_pages = PAGES_PER_STEP
    while n_pages % g_pages:
        g_pages //= 2
    n_rows = qa.shape[1]
    t_len = n_rows // N_HEADS
    grid = (nb, n_pages // g_pages)
    per_b = lambda shape: pl.BlockSpec((None,) + shape, lambda b, j, pt: (b,) + (0,) * len(shape))

    kv_page = (ATT_W, PAGE_SIZE)
    lf_page = (N_HEADS, PAGE_SIZE)
    in_hbm = pl.BlockSpec(memory_space=pl.ANY)
    grid_spec = pltpu.PrefetchScalarGridSpec(
        num_scalar_prefetch=1,
        grid=grid,
        in_specs=[per_b((n_rows, PAGED_K)), per_b(kv_page), per_b(kv_page), per_b(lf_page),
                  in_hbm, in_hbm, in_hbm],
        out_specs=per_b((t_len, ATT_W)),
        scratch_shapes=[pltpu.VMEM((n_rows, LANES), F32), pltpu.VMEM((n_rows, LANES), F32),
                        pltpu.VMEM((n_rows, ATT_W), F32), pltpu.VMEM((N_HEADS, LANES), F32),
                        pltpu.VMEM((2, g_pages) + kv_page, F32),
                        pltpu.VMEM((2, g_pages) + kv_page, F32),
                        pltpu.VMEM((2, g_pages) + lf_page, F32),
                        pltpu.SemaphoreType.DMA((2,))],
    )
    return pl.pallas_call(
        functools.partial(_paged_kernel, layer=layer),
        grid_spec=grid_spec,
        out_shape=jax.ShapeDtypeStruct((nb, t_len, ATT_W), F32),
        compiler_params=pltpu.CompilerParams(
            dimension_semantics=("arbitrary", "arbitrary"), vmem_limit_bytes=VMEM_LIMIT),
        name="paged",
    )(page_table, qa, k_new, v_new, cum_new, cache_kt, cache_vt, cache_lft)


def _post_kernel(x_ref, att_ref, g0_ref, mp_ref, wba_ref, wout_ref, g2_ref, wg_ref, wu_ref,
                 wdown_ref, gf_ref, y_ref, *, final_norm):
    merged = g0_ref[...].astype(F32) * _dot(att_ref[...], wba_ref[...]) + mp_ref[...]
    h = x_ref[...] + _dot(merged.astype(BF16), wout_ref[...])
    hn = _rmsnorm(h, g2_ref[...]).astype(BF16)
    d_ff = wg_ref.shape[1]
    y = h
    for c in range(0, d_ff, FFN_CHUNK):
        g = _dot(hn, wg_ref[:, c:c + FFN_CHUNK])
        u = _dot(hn, wu_ref[:, c:c + FFN_CHUNK])
        act = (g * jax.nn.sigmoid(g) * u).astype(BF16)
        y = y + _dot(act, wdown_ref[c:c + FFN_CHUNK, :])
    if final_norm:
        y = _rmsnorm(y, gf_ref[...])
    y_ref[...] = y


def _post_call(x, att, g0, mpart, weights, rows, final_norm):
    nb, t_len, d = x.shape
    grid = (nb, t_len // rows)
    row_spec = lambda c: pl.BlockSpec((None, rows, c), lambda b, i: (b, i, 0))
    full = lambda a: pl.BlockSpec(a.shape, lambda b, i: (0,) * a.ndim,
                                  pipeline_mode=pl.Buffered(1))
    return pl.pallas_call(
        functools.partial(_post_kernel, final_norm=final_norm),
        grid=grid,
        in_specs=[row_spec(d), row_spec(ATT_W), row_spec(d), row_spec(d)]
                 + [full(w) for w in weights],
        out_specs=row_spec(d),
        out_shape=jax.ShapeDtypeStruct((nb, t_len, d), F32),
        compiler_params=pltpu.CompilerParams(
            dimension_semantics=("arbitrary", "arbitrary"), vmem_limit_bytes=VMEM_LIMIT),
        name="post",
    )(x, att, g0, mpart, *weights)


def _block_diag(w):
    n, d, e = w.shape
    eye = jnp.eye(n, dtype=w.dtype)
    return (w[:, :, None, :] * eye[:, None, :, None]).reshape(n * d, n * e)


def _row2(v):
    return v.reshape(1, -1).astype(F32)


def _history_rows(buf, t_len):
    nb, w1, c = buf.shape
    padded = jnp.concatenate([jnp.zeros((nb, t_len - w1, c), buf.dtype), buf], axis=1)
    return jnp.roll(padded, -1, axis=0).reshape(1, nb * t_len, c)


def _f_placement():
    place = np.zeros((F_PARTS * LANES, N_PAIRS * LANES), np.float32)
    for k in range(F_PARTS):
        for h in range(N_HEADS):
            place[k * LANES + h, (h // 2) * LANES + F_PARTS * (h % 2) + k] = 1.0
    return jnp.asarray(place, BF16)


def _paged_queries(q, db, t_dec):
    q4 = jnp.transpose(q.reshape(db, t_dec, N_HEADS, HEAD_DIM), (0, 2, 1, 3))
    eye = jnp.eye(N_HEADS, dtype=q.dtype)
    q_bd = (q4[:, :, :, None, :] * eye[None, :, None, :, None]).reshape(
        db, N_HEADS * t_dec, ATT_W)
    sel = -jnp.repeat(eye, t_dec, axis=0)
    aug = jnp.concatenate(
        [sel, sel, jnp.zeros((N_HEADS * t_dec, PAGED_K - ATT_W - 2 * N_HEADS), q.dtype)], axis=1)
    return jnp.concatenate([q_bd, jnp.broadcast_to(aug, (db,) + aug.shape)], axis=2)


def kernel(x_prompt, x_sample, cache_k, cache_v, cache_logf, state_lru_h, state_lru_conv,
           state_sc_conv, page_table, norm1_g, w_in, b_f, lru_conv_w, lru_conv_b, lru_w_a,
           lru_b_a, lru_w_x, lru_b_x, lru_lambda, sc_conv_w, w_gate, b_gate, w_br_att,
           w_br_lru, w_br_sc, w_out, norm2_g, w_ffn_in, w_ffn_out, final_norm_g):
    depth = w_in.shape[0]
    bp, s_len, d = x_prompt.shape
    db, t_dec, _ = x_sample.shape
    n_tok_s = db * t_dec
    d_ff = w_ffn_out.shape[1]
    rows_p = min(ROW_TILE, s_len)
    n_pool = cache_k.shape[1]

    cache_kt = jnp.transpose(cache_k, (0, 1, 3, 4, 2)).reshape(depth, n_pool, ATT_W, PAGE_SIZE)
    cache_vt = jnp.transpose(cache_v, (0, 1, 3, 4, 2)).reshape(depth, n_pool, ATT_W, PAGE_SIZE)
    cache_lft = jnp.transpose(cache_logf, (0, 1, 3, 2))
    place = _f_placement()

    yp = x_prompt
    ys = x_sample.reshape(1, n_tok_s, d)
    outs_p = [[] for _ in range(3)]
    outs_s = [[] for _ in range(6)]
    kv_all = None
    q_off = 3 * ATT_W
    r_off = q_off + N_HEADS
    for l in range(depth):
        wi = w_in[l]
        wqkv = wi[:, :q_off].astype(BF16)
        wfl = jnp.pad(wi[:, q_off:r_off], ((0, 0), (0, LANES - N_HEADS))).astype(BF16)
        bfl = jnp.pad(b_f[l].reshape(1, -1), ((0, 0), (0, LANES - N_HEADS))).astype(F32)
        g1 = _row2(norm1_g[l])
        mix_w = [g1, wi[:, r_off:].astype(BF16), lru_conv_w[l], _row2(lru_conv_b[l]),
                 _block_diag(lru_w_a[l]).astype(BF16), _row2(lru_b_a[l]),
                 _block_diag(lru_w_x[l]).astype(BF16), _row2(lru_b_x[l]), _row2(lru_lambda[l]),
                 sc_conv_w[l], w_gate[l].astype(BF16), _row2(b_gate[l]),
                 w_br_lru[l].astype(BF16), w_br_sc[l].astype(BF16)]
        post_w = [w_br_att[l].astype(BF16), w_out[l].astype(BF16), _row2(norm2_g[l]),
                  w_ffn_in[l][:, :d_ff].astype(BF16), w_ffn_in[l][:, d_ff:].astype(BF16),
                  w_ffn_out[l].astype(BF16), _row2(final_norm_g)]
        last = l == depth - 1

        q, k2, vt, *kv_all = _qkv_prompt_call(yp, g1, wqkv, wfl, bfl, place, rows_p, l, depth,
                                              kv_all)
        g0, mpart, xr_t, u_t, h_t = _mix_call(yp, mix_w, rows_p, rows_p)
        att = _flash_call(q, k2, vt)
        yp = _post_call(yp, att, g0, mpart, post_w, rows_p, last)
        outs_p[0].append(h_t[:, SUBLANES - 1])
        outs_p[1].append(xr_t[:, SUBLANES - (LRU_CONV - 1):])
        outs_p[2].append(u_t[:, SUBLANES - (SC_CONV - 1):])

        q, k, v, kb, vb, lf, fc = _qkv_sample_call(ys[0], g1, wqkv, wfl, bfl, t_dec)
        state = (_history_rows(state_lru_conv[l], t_dec), _history_rows(state_sc_conv[l], t_dec),
                 jnp.repeat(state_lru_h[l], t_dec, axis=0).reshape(1, n_tok_s, LRU_W))
        g0, mpart, xr_t, u_t, h_t = _mix_call(ys, mix_w, n_tok_s, t_dec, state)
        as_page = lambda a: jnp.pad(jnp.transpose(a.reshape(db, t_dec, -1), (0, 2, 1)),
                                    ((0, 0), (0, 0), (0, PAGE_SIZE - t_dec)))
        att = _paged_call(l, page_table, _paged_queries(q, db, t_dec), as_page(kb), as_page(vb),
                          as_page(fc), cache_kt, cache_vt, cache_lft)
        att = att.reshape(1, n_tok_s, ATT_W).astype(BF16)
        ys = _post_call(ys, att, g0, mpart, post_w, n_tok_s, last)
        outs_s[0].append(k.reshape(db, t_dec, N_HEADS, HEAD_DIM))
        outs_s[1].append(v.reshape(db, t_dec, N_HEADS, HEAD_DIM))
        outs_s[2].append(lf.reshape(db, t_dec, N_HEADS))
        outs_s[3].append(h_t.reshape(db, t_dec, LRU_W)[:, t_dec - 1])
        outs_s[4].append(xr_t.reshape(db, t_dec, LRU_W)[:, t_dec - (LRU_CONV - 1):])
        outs_s[5].append(u_t.reshape(db, t_dec, SC_W)[:, t_dec - (SC_CONV - 1):])

    kv_out = lambda o: jnp.transpose(
        o.reshape(depth, bp, N_HEADS, HEAD_DIM, s_len), (0, 1, 4, 2, 3))
    return (yp, ys.reshape(db, t_dec, d),
            kv_out(kv_all[0]), kv_out(kv_all[1]), jnp.transpose(kv_all[2], (0, 1, 3, 2)),
            *[jnp.stack(o) for o in outs_p], *[jnp.stack(o) for o in outs_s])
```

```python
import functools
import math

import numpy as np
import jax
import jax.numpy as jnp
from jax import lax
from jax.experimental import pallas as pl
from jax.experimental.pallas import tpu as pltpu

N_HEADS = 8
HEAD_DIM = 64
ATT_W = N_HEADS * HEAD_DIM
LRU_W = 512
LRU_BLOCKS = 8
LRU_CONV = 4
LRU_C = 8.0
SC_W = 512
SC_CONV = 3
N_BRANCH = 3
RMS_EPS = 1e-6
NEG_INF = -1e30
PAGE_SIZE = 128
LOG2E = math.log2(math.e)

LANES = 128
SUBLANES = 8
BF16_ROWS = 16
VMEM_LIMIT = 60 * 1024 * 1024
ROW_TILE = 512
ATT_BLOCK = ROW_TILE
FFN_CHUNK = 256
PAGES_PER_STEP = 16
N_PAIRS = ATT_W // LANES
F_PARTS = 3
VT_ROWS = LANES + BF16_ROWS
PAIRS_PER_STEP = 4
KEY_SUB_BLOCK = 512
PAGED_K = ATT_W + LANES

F32 = jnp.float32
BF16 = jnp.bfloat16


def _dot(a, b):
    return jnp.dot(a, b, preferred_element_type=F32)


def _dot_nt(a, b):
    return lax.dot_general(a, b, (((1,), (1,)), ((), ())), preferred_element_type=F32)


def _rmsnorm(x, g):
    ms = jnp.mean(x * x, axis=-1, keepdims=True)
    return x * lax.rsqrt(ms + RMS_EPS) * g


def _softplus(x):
    return jnp.maximum(x, 0.0) + jnp.log1p(jnp.exp(-jnp.abs(x)))


def _bf16_pieces(x, n):
    pieces = []
    for _ in range(n - 1):
        p = x.astype(BF16)
        pieces.append(p)
        x = x - p.astype(F32)
    pieces.append(x.astype(BF16))
    return pieces


def _row_in_segment(shape, seg):
    return lax.broadcasted_iota(jnp.int32, shape, 0) & (seg - 1)


def _segment_cumsum(x, seg):
    t = _row_in_segment(x.shape, seg)
    s = 1
    while s < seg:
        x = x + jnp.where(t >= s, pltpu.roll(x, s, 0), 0.0)
        s *= 2
    return x


def _segment_linear_scan(a, b, seg):
    t = _row_in_segment(a.shape, seg)
    s = 1
    while s < seg:
        keep = t >= s
        a_prev = jnp.where(keep, pltpu.roll(a, s, 0), 1.0)
        b_prev = jnp.where(keep, pltpu.roll(b, s, 0), 0.0)
        b = a * b_prev + b
        a = a * a_prev
        s *= 2
    return a, b


def _causal_conv(u, hist, w_ref, seg):
    width = w_ref.shape[0]
    rows = u.shape[0]
    y = u * w_ref[width - 1:width, :]
    t = _row_in_segment(hist.shape, SUBLANES if seg == rows else seg)
    for d in range(1, width):
        rolled = pltpu.roll(u, d, 0)
        if seg == rows:
            head = jnp.where(t >= d, rolled[:SUBLANES, :], pltpu.roll(hist, d, 0))
            shifted = jnp.concatenate([head, rolled[SUBLANES:, :]], axis=0)
        else:
            shifted = jnp.where(t >= d, rolled, pltpu.roll(hist, d, 0))
        y = y + shifted * w_ref[width - 1 - d:width - d, :]
    return y


def _linear_scan(a, b, h_in, seg):
    rows = a.shape[0]
    if seg != rows or rows == SUBLANES:
        a_cum, b_cum = _segment_linear_scan(a, b, seg)
        return a_cum * h_in + b_cum
    a_cum, b_cum = _segment_linear_scan(a, b, SUBLANES)
    carry = h_in
    groups = []
    for r0 in range(0, rows, SUBLANES):
        hg = a_cum[r0:r0 + SUBLANES, :] * carry + b_cum[r0:r0 + SUBLANES, :]
        groups.append(hg)
        carry = hg[SUBLANES - 1:, :]
    return jnp.concatenate(groups, axis=0)


def _qkv_project(x_ref, g_ref, wqkv_ref, wfl_ref, bf_ref):
    xn = _rmsnorm(x_ref[...], g_ref[...]).astype(BF16)
    z = _dot(xn, wqkv_ref[...])
    fl = _dot(xn, wfl_ref[...]) + bf_ref[...]
    lf = -_softplus(-fl)
    return z[:, :ATT_W], z[:, ATT_W:2 * ATT_W], z[:, 2 * ATT_W:], lf


def _qkv_prompt_kernel(x_ref, g_ref, wqkv_ref, wfl_ref, bf_ref, place_ref, *refs):
    q_ref, k2_ref, vt_ref, ktf_ref, vtf_ref, lft_ref, carry_ref = refs[-7:]
    rows = x_ref.shape[0]
    q, k, v, lf = _qkv_project(x_ref, g_ref, wqkv_ref, wfl_ref, bf_ref)
    q_ref[...] = (q * (HEAD_DIM ** -0.5 * LOG2E)).astype(BF16)
    ktf_ref[...] = k.T
    vt = v.T
    vtf_ref[...] = vt
    vtb = vt.astype(BF16)
    ones = jnp.ones((BF16_ROWS, rows), BF16)
    for p in range(N_PAIRS):
        vt_ref[p] = jnp.concatenate([vtb[p * LANES:(p + 1) * LANES, :], ones], axis=0)
    lft_ref[...] = lf.T[:N_HEADS, :]

    @pl.when(pl.program_id(1) == 0)
    def _():
        carry_ref[...] = jnp.zeros_like(carry_ref)

    cum = _segment_cumsum(lf, rows) + carry_ref[SUBLANES - 1:SUBLANES, :]
    carry_ref[...] = cum[rows - SUBLANES:, :]
    pieces = jnp.concatenate(_bf16_pieces(cum * LOG2E, F_PARTS), axis=1)
    placed = _dot(pieces, place_ref[...]).astype(BF16)
    kb = k.astype(BF16)
    for p in range(N_PAIRS):
        k2_ref[:, 2 * p * LANES:(2 * p + 1) * LANES] = kb[:, p * LANES:(p + 1) * LANES]
        k2_ref[:, (2 * p + 1) * LANES:(2 * p + 2) * LANES] = placed[:, p * LANES:(p + 1) * LANES]


def _qkv_prompt_call(x, g, wqkv, wfl, bfl, place, rows, layer, depth, prev):
    nb, s_len, d = x.shape
    n_t = s_len // rows
    row_spec = lambda c: pl.BlockSpec((None, rows, c), lambda b, i: (b, i, 0))
    col_spec = lambda r: pl.BlockSpec((None, None, r, rows), lambda b, i: (layer, b, 0, i))
    n_in = 6
    prev = () if prev is None else tuple(prev)
    full = lambda a: pl.BlockSpec(a.shape, lambda b, i: (0,) * a.ndim,
                                  pipeline_mode=pl.Buffered(1))
    out_shape = (
        jax.ShapeDtypeStruct((nb, s_len, ATT_W), BF16),
        jax.ShapeDtypeStruct((nb, s_len, 2 * ATT_W), BF16),
        jax.ShapeDtypeStruct((nb, N_PAIRS, n_t, VT_ROWS, rows), BF16),
        jax.ShapeDtypeStruct((depth, nb, ATT_W, s_len), F32),
        jax.ShapeDtypeStruct((depth, nb, ATT_W, s_len), F32),
        jax.ShapeDtypeStruct((depth, nb, N_HEADS, s_len), F32),
    )
    return pl.pallas_call(
        _qkv_prompt_kernel,
        grid=(nb, n_t),
        in_specs=[row_spec(d), full(g), full(wqkv), full(wfl), full(bfl), full(place)]
                 + [pl.BlockSpec(memory_space=pl.ANY)] * len(prev),
        input_output_aliases={n_in + k: 3 + k for k in range(len(prev))},
        out_specs=(row_spec(ATT_W), row_spec(2 * ATT_W),
                   pl.BlockSpec((None, N_PAIRS, None, VT_ROWS, rows), lambda b, i: (b, 0, i, 0, 0)),
                   col_spec(ATT_W), col_spec(ATT_W), col_spec(N_HEADS)),
        out_shape=out_shape,
        scratch_shapes=[pltpu.VMEM((SUBLANES, LANES), F32)],
        compiler_params=pltpu.CompilerParams(
            dimension_semantics=("arbitrary", "arbitrary"), vmem_limit_bytes=VMEM_LIMIT),
        name="qkv_prompt",
    )(x, g, wqkv, wfl, bfl, place, *prev)


def _qkv_sample_kernel(x_ref, g_ref, wqkv_ref, wfl_ref, bf_ref,
                       q_ref, k_ref, v_ref, kb_ref, vb_ref, lf_ref, fc_ref, *, seg):
    q, k, v, lf = _qkv_project(x_ref, g_ref, wqkv_ref, wfl_ref, bf_ref)
    q_ref[...] = (q * (HEAD_DIM ** -0.5)).astype(BF16)
    k_ref[...] = k
    v_ref[...] = v
    kb_ref[...] = k.astype(BF16)
    vb_ref[...] = v.astype(BF16)
    lf_ref[...] = lf[:, :N_HEADS]
    fc_ref[...] = _segment_cumsum(lf, seg)[:, :N_HEADS]


def _qkv_sample_call(x, g, wqkv, wfl, bfl, seg):
    rows, d = x.shape
    full = lambda a: pl.BlockSpec(a.shape, lambda i: (0,) * a.ndim)
    sds = lambda c, dt: jax.ShapeDtypeStruct((rows, c), dt)
    out_shape = (sds(ATT_W, BF16), sds(ATT_W, F32), sds(ATT_W, F32), sds(ATT_W, BF16),
                 sds(ATT_W, BF16), sds(N_HEADS, F32), sds(N_HEADS, F32))
    return pl.pallas_call(
        functools.partial(_qkv_sample_kernel, seg=seg),
        grid=(1,),
        in_specs=[full(x), full(g), full(wqkv), full(wfl), full(bfl)],
        out_specs=tuple(full(o) for o in out_shape),
        out_shape=out_shape,
        compiler_params=pltpu.CompilerParams(
            dimension_semantics=("arbitrary",), vmem_limit_bytes=VMEM_LIMIT),
        name="qkv_sample",
    )(x, g, wqkv, wfl, bfl)


def _mix_body(x_ref, g_ref, wrest_ref, cw_ref, cb_ref, wa_ref, ba_ref, wx_ref, bx_ref, lam_ref,
              scw_ref, wgate_ref, bgate_ref, wbl_ref, wbs_ref,
              hist_lru, hist_sc, h_in,
              g0_ref, mp_ref, xr_tail_ref, u_tail_ref, h_tail_ref, *, seg):
    x = x_ref[...]
    rows = x.shape[0]
    xn = _rmsnorm(x, g_ref[...]).astype(BF16)
    z = _dot(xn, wrest_ref[...])
    xr = z[:, :LRU_W]
    bg = z[:, LRU_W:LRU_W + SC_W]
    cg = z[:, LRU_W + SC_W:LRU_W + 2 * SC_W]
    hs = z[:, LRU_W + 2 * SC_W:]

    xc = _causal_conv(xr, hist_lru, cw_ref, seg) + cb_ref[...]
    xcb = xc.astype(BF16)
    half = LRU_W // 2
    bd = lambda w_ref: jnp.concatenate(
        [_dot(xcb[:, :half], w_ref[:half, :half]), _dot(xcb[:, half:], w_ref[half:, half:])],
        axis=1)
    r = jax.nn.sigmoid(bd(wa_ref) + ba_ref[...])
    gate_i = jax.nn.sigmoid(bd(wx_ref) + bx_ref[...])
    log_a = (-LRU_C) * r * _softplus(-lam_ref[...])
    a = jnp.exp(log_a)
    mult = jnp.sqrt(-jnp.tanh(log_a) * (a * a + 1.0))
    h = _linear_scan(a, mult * gate_i * xc, h_in, seg)

    u = cg * hs
    sc = bg * _causal_conv(u, hist_sc, scw_ref, seg)

    d = x.shape[1]
    gate = lambda j: jax.nn.sigmoid(
        _dot(xn, wgate_ref[:, j * d:(j + 1) * d]) + bgate_ref[:, j * d:(j + 1) * d])
    g0_ref[...] = gate(0).astype(BF16)
    mp_ref[...] = (gate(1) * _dot(h.astype(BF16), wbl_ref[...])
                   + gate(2) * _dot(sc.astype(BF16), wbs_ref[...]))
    tail = xr_tail_ref.shape[0]
    xr_tail_ref[...] = xr[rows - tail:, :]
    u_tail_ref[...] = u[rows - tail:, :]
    h_tail_ref[...] = h[rows - tail:, :]
    return xr, u, h


def _mix_kernel_fresh(*refs, seg):
    ins, outs, (hist_lru_ref, hist_sc_ref, h_ref) = refs[:15], refs[15:20], refs[20:]

    @pl.when(pl.program_id(1) == 0)
    def _():
        hist_lru_ref[...] = jnp.zeros_like(hist_lru_ref)
        hist_sc_ref[...] = jnp.zeros_like(hist_sc_ref)
        h_ref[...] = jnp.zeros_like(h_ref)

    xr, u, h = _mix_body(*ins, hist_lru_ref[...], hist_sc_ref[...],
                         h_ref[SUBLANES - 1:SUBLANES, :], *outs, seg=seg)
    tail = h.shape[0] - SUBLANES
    hist_lru_ref[...] = xr[tail:, :]
    hist_sc_ref[...] = u[tail:, :]
    h_ref[...] = h[tail:, :]


def _mix_kernel_state(*refs, seg):
    ins, (hist_lru_ref, hist_sc_ref, h_ref), outs = refs[:15], refs[15:18], refs[18:]
    _mix_body(*ins, hist_lru_ref[...], hist_sc_ref[...], h_ref[...], *outs, seg=seg)


def _mix_call(x, weights, rows, seg, state=None):
    nb, t_len, d = x.shape
    grid = (nb, t_len // rows)
    row_spec = lambda c: pl.BlockSpec((None, rows, c), lambda b, i: (b, i, 0))
    full = lambda a: pl.BlockSpec(a.shape, lambda b, i: (0,) * a.ndim,
                                  pipeline_mode=pl.Buffered(1))
    tail = SUBLANES if state is None else rows
    n_tail = 1 if state is None else t_len // rows
    tail_spec = pl.BlockSpec((None, tail, LRU_W), lambda b, i: (b, i if state is not None else 0, 0))
    out_shape = (
        jax.ShapeDtypeStruct((nb, t_len, d), BF16),
        jax.ShapeDtypeStruct((nb, t_len, d), F32),
        jax.ShapeDtypeStruct((nb, n_tail * tail, LRU_W), F32),
        jax.ShapeDtypeStruct((nb, n_tail * tail, SC_W), F32),
        jax.ShapeDtypeStruct((nb, n_tail * tail, LRU_W), F32),
    )
    in_specs = [row_spec(d)] + [full(w) for w in weights]
    args = [x, *weights]
    if state is None:
        kern = functools.partial(_mix_kernel_fresh, seg=seg)
        scratch = [pltpu.VMEM((SUBLANES, LRU_W), F32), pltpu.VMEM((SUBLANES, SC_W), F32),
                   pltpu.VMEM((SUBLANES, LRU_W), F32)]
    else:
        kern = functools.partial(_mix_kernel_state, seg=seg)
        scratch = []
        in_specs += [row_spec(LRU_W), row_spec(SC_W), row_spec(LRU_W)]
        args += list(state)
    return pl.pallas_call(
        kern,
        grid=grid,
        in_specs=in_specs,
        out_specs=(row_spec(d), row_spec(d), tail_spec, tail_spec, tail_spec),
        out_shape=out_shape,
        scratch_shapes=scratch,
        compiler_params=pltpu.CompilerParams(
            dimension_semantics=("arbitrary", "arbitrary"), vmem_limit_bytes=VMEM_LIMIT),
        name="mix",
    )(*args)


def _flash_kernel(q_ref, k2_ref, vt_ref, sel_ref, o_ref, m_ref, acc_ref, s0_ref, s1_ref):
    blk = q_ref.shape[0]
    n_chain = vt_ref.shape[0]
    i = pl.program_id(2)
    sel = sel_ref[...]
    qabs = []
    for c in range(n_chain):
        q2 = q_ref[:, c * LANES:(c + 1) * LANES].astype(F32)
        f_lanes = lambda e: jnp.broadcast_to(sel[2 + e:3 + e, :], q2.shape)
        qabs.append(jnp.concatenate(
            [jnp.concatenate([q2 * sel[e:e + 1, :], f_lanes(e)], axis=1) for e in range(2)],
            axis=0).T.astype(BF16))

    m_ref[...] = jnp.full_like(m_ref, NEG_INF)
    acc_ref[...] = jnp.zeros_like(acc_ref)

    sub = s0_ref.shape[2]

    def score_block(j, s_ref):
        start = pl.multiple_of(j * blk, blk)
        for c in range(n_chain):
            for si in range(blk // sub):
                k2 = k2_ref[pl.ds(start + si * sub, sub), 2 * c * LANES:2 * (c + 1) * LANES]
                s_ref[c, si] = _dot(k2, qabs[c])

    def softmax_block(j, s_ref, masked):
        for c in range(n_chain):
            m_run = m_ref[c]
            for si in range(blk // sub):
                st = s_ref[c, si]
                if masked:
                    key = lax.broadcasted_iota(jnp.int32, st.shape, 0) + si * sub
                    qry = lax.broadcasted_iota(jnp.int32, st.shape, 1) & (blk - 1)
                    st = jnp.where(key <= qry, st, NEG_INF)
                m_new = jnp.maximum(m_run, jnp.max(st, axis=0, keepdims=True))
                alpha = jnp.exp2(m_run - m_new)
                p_ = jnp.exp2((st - m_new).astype(BF16))
                acc_ref[c] = alpha * acc_ref[c] + _dot(
                    vt_ref[c, j, :, si * sub:(si + 1) * sub], p_)
                m_run = m_new
            m_ref[c] = m_run

    score_block(0, s0_ref)

    def body(t, carry):
        j = 2 * t
        score_block(j + 1, s1_ref)
        softmax_block(j, s0_ref, False)
        score_block(j + 2, s0_ref)
        softmax_block(j + 1, s1_ref, False)
        return carry

    lax.fori_loop(0, i // 2, body, 0)

    @pl.when(i % 2 == 0)
    def _():
        softmax_block(i, s0_ref, True)

    @pl.when(i % 2 == 1)
    def _():
        score_block(i, s1_ref)
        softmax_block(i - 1, s0_ref, False)
        softmax_block(i, s1_ref, True)

    row = lax.broadcasted_iota(jnp.int32, (LANES, blk), 0)
    for c in range(n_chain):
        acc = acc_ref[c]
        out = acc[:LANES, :] / acc[LANES:LANES + 1, :]
        out = jnp.where(row < HEAD_DIM, out[:, :blk], out[:, blk:])
        o_ref[:, c * LANES:(c + 1) * LANES] = out.T.astype(o_ref.dtype)


def _flash_call(q, k2, vt):
    nb, s_len, _ = q.shape
    blk = vt.shape[-1]
    n_chain = PAIRS_PER_STEP
    grid = (nb, N_PAIRS // n_chain, s_len // blk)
    sub = min(KEY_SUB_BLOCK, blk)
    score_shape = (n_chain, blk // sub, sub, 2 * blk)
    sel = np.zeros((SUBLANES, LANES), np.float32)
    sel[0, :HEAD_DIM] = 1.0
    sel[1, HEAD_DIM:] = 1.0
    sel[2, :F_PARTS] = -1.0
    sel[3, F_PARTS:2 * F_PARTS] = -1.0
    sel = jnp.asarray(sel)
    return pl.pallas_call(
        _flash_kernel,
        grid=grid,
        in_specs=[
            pl.BlockSpec((None, blk, n_chain * LANES), lambda b, p, i: (b, i, p)),
            pl.BlockSpec((None, s_len, 2 * n_chain * LANES), lambda b, p, i: (b, 0, p),
                         pipeline_mode=pl.Buffered(1)),
            pl.BlockSpec((None, n_chain, s_len // blk, VT_ROWS, blk),
                         lambda b, p, i: (b, p, 0, 0, 0), pipeline_mode=pl.Buffered(1)),
            pl.BlockSpec(sel.shape, lambda b, p, i: (0, 0)),
        ],
        out_specs=pl.BlockSpec((None, blk, n_chain * LANES), lambda b, p, i: (b, i, p)),
        out_shape=jax.ShapeDtypeStruct((nb, s_len, ATT_W), BF16),
        scratch_shapes=[pltpu.VMEM((n_chain, 1, 2 * blk), F32),
                        pltpu.VMEM((n_chain, VT_ROWS, 2 * blk), F32),
                        pltpu.VMEM(score_shape, F32), pltpu.VMEM(score_shape, F32)],
        compiler_params=pltpu.CompilerParams(
            dimension_semantics=("arbitrary", "arbitrary", "arbitrary"),
            vmem_limit_bytes=VMEM_LIMIT),
        name="flash",
    )(q, k2, vt, sel)


def _paged_kernel(pt_ref, qa_ref, kn_ref, vn_ref, cn_ref, ck_ref, cv_ref, clf_ref,
                  o_ref, m_ref, l_ref, acc_ref, fc_ref, kbuf, vbuf, lfbuf, sem, *, layer):
    n_pages = kbuf.shape[1]
    b = pl.program_id(0)
    j = pl.program_id(1)
    n_j = pl.num_programs(1)
    step = b * n_j + j
    slot = lax.rem(step, 2)
    n_rows = qa_ref.shape[0]
    t_len = n_rows // N_HEADS

    def page_copies(bb, jj, slot_):
        copies = []
        for g in range(n_pages):
            page = pt_ref[bb, jj * n_pages + g]
            for src, dst in ((ck_ref, kbuf), (cv_ref, vbuf), (clf_ref, lfbuf)):
                copies.append(pltpu.make_async_copy(
                    src.at[layer, page], dst.at[slot_, g], sem.at[slot_]))
        return copies

    @pl.when(step == 0)
    def _():
        for c in page_copies(b, j, slot):
            c.start()

    @pl.when(step + 1 < pl.num_programs(0) * n_j)
    def _():
        wrap = j + 1 == n_j
        for c in page_copies(jnp.where(wrap, b + 1, b), jnp.where(wrap, 0, j + 1), 1 - slot):
            c.start()

    for c in page_copies(b, j, slot):
        c.wait()
    k_refs = [kbuf.at[slot, g] for g in range(n_pages)]
    v_refs = [vbuf.at[slot, g] for g in range(n_pages)]
    lf_refs = [lfbuf.at[slot, g] for g in range(n_pages)]

    @pl.when(j == 0)
    def _():
        m_ref[...] = jnp.full_like(m_ref, NEG_INF)
        l_ref[...] = jnp.zeros_like(l_ref)
        acc_ref[...] = jnp.zeros_like(acc_ref)
        fc_ref[...] = jnp.zeros_like(fc_ref)

    q_bd = qa_ref[:, :ATT_W]
    q_sel = qa_ref[:, ATT_W:ATT_W + BF16_ROWS]

    def scores(k_page, cum):
        hi, lo = _bf16_pieces(cum, 2)
        pieces = jnp.concatenate([hi.astype(F32), lo.astype(F32)], axis=0).astype(BF16)
        return _dot(q_bd, k_page) + _dot(q_sel, pieces)

    def update(s_list, v_list):
        s_all = jnp.concatenate(s_list, axis=1)
        v_all = jnp.concatenate(v_list, axis=1)
        m_old = m_ref[...][:, :1]
        m_new = jnp.maximum(m_old, jnp.max(s_all, axis=1, keepdims=True))
        alpha = jnp.exp(m_old - m_new)
        p_ = jnp.exp(s_all - m_new)
        l_new = alpha * l_ref[...][:, :1] + jnp.sum(p_, axis=1, keepdims=True)
        acc_ref[...] = alpha * acc_ref[...] + _dot_nt(p_.astype(BF16), v_all)
        m_ref[...] = jnp.broadcast_to(m_new, m_ref.shape)
        l_ref[...] = jnp.broadcast_to(l_new, l_ref.shape)

    r_i = lax.broadcasted_iota(jnp.int32, (PAGE_SIZE, PAGE_SIZE), 0)
    c_i = lax.broadcasted_iota(jnp.int32, (PAGE_SIZE, PAGE_SIZE), 1)
    tri = jnp.where(r_i <= c_i, 1.0, 0.0).astype(BF16)
    lf_pages = [r[...] for r in lf_refs]
    lf_hi, lf_lo = _bf16_pieces(jnp.concatenate(lf_pages, axis=0), 2)
    cum_local = _dot(lf_hi, tri) + _dot(lf_lo, tri)

    carry = fc_ref[:, :1]
    s_list = []
    for g in range(n_pages):
        cum = cum_local[g * N_HEADS:(g + 1) * N_HEADS, :] + carry
        carry = carry + jnp.sum(lf_pages[g], axis=1, keepdims=True)
        s_list.append(scores(k_refs[g][...].astype(BF16), cum))
    fc_ref[...] = jnp.broadcast_to(carry, fc_ref.shape)
    update(s_list, [r[...].astype(BF16) for r in v_refs])

    @pl.when(j == pl.num_programs(1) - 1)
    def _():
        s = scores(kn_ref[...], carry + cn_ref[...])
        qry = lax.broadcasted_iota(jnp.int32, s.shape, 0) & (t_len - 1)
        pos = lax.broadcasted_iota(jnp.int32, s.shape, 1)
        update([jnp.where(pos <= qry, s, NEG_INF)], [vn_ref[...]])
        out = acc_ref[...] / l_ref[...][:, :1]
        for h in range(N_HEADS):
            o_ref[:, h * HEAD_DIM:(h + 1) * HEAD_DIM] = (
                out[h * t_len:(h + 1) * t_len, h * HEAD_DIM:(h + 1) * HEAD_DIM])


def _paged_call(layer, page_table, qa, k_new, v_new, cum_new, cache_kt, cache_vt, cache_lft):
    nb, n_pages = page_table.shape
    g_pages = PAGES_PER_STEP
    while n_pages % g_pages:
        g_pages //= 2
    n_rows = qa.shape[1]
    t_len = n_rows // N_HEADS
    grid = (nb, n_pages // g_pages)
    per_b = lambda shape: pl.BlockSpec((None,) + shape, lambda b, j, pt: (b,) + (0,) * len(shape))

    kv_page = (ATT_W, PAGE_SIZE)
    lf_page = (N_HEADS, PAGE_SIZE)
    in_hbm = pl.BlockSpec(memory_space=pl.ANY)
    grid_spec = pltpu.PrefetchScalarGridSpec(
        num_scalar_prefetch=1,
        grid=grid,
        in_specs=[per_b((n_rows, PAGED_K)), per_b(kv_page), per_b(kv_page), per_b(lf_page),
                  in_hbm, in_hbm, in_hbm],
        out_specs=per_b((t_len, ATT_W)),
        scratch_shapes=[pltpu.VMEM((n_rows, LANES), F32), pltpu.VMEM((n_rows, LANES), F32),
                        pltpu.VMEM((n_rows, ATT_W), F32), pltpu.VMEM((N_HEADS, LANES), F32),
                        pltpu.VMEM((2, g_pages) + kv_page, F32),
                        pltpu.VMEM((2, g_pages) + kv_page, F32),
                        pltpu.VMEM((2, g_pages) + lf_page, F32),
                        pltpu.SemaphoreType.DMA((2,))],
    )
    return pl.pallas_call(
        functools.partial(_paged_kernel, layer=layer),
        grid_spec=grid_spec,
        out_shape=jax.ShapeDtypeStruct((nb, t_len, ATT_W), F32),
        compiler_params=pltpu.CompilerParams(
            dimension_semantics=("arbitrary", "arbitrary"), vmem_limit_bytes=VMEM_LIMIT),
        name="paged",
    )(page_table, qa, k_new, v_new, cum_new, cache_kt, cache_vt, cache_lft)


def _post_kernel(x_ref, att_ref, g0_ref, mp_ref, wba_ref, wout_ref, g2_ref, wg_ref, wu_ref,
                 wdown_ref, gf_ref, y_ref, *, final_norm):
    merged = g0_ref[...].astype(F32) * _dot(att_ref[...], wba_ref[...]) + mp_ref[...]
    h = x_ref[...] + _dot(merged.astype(BF16), wout_ref[...])
    hn = _rmsnorm(h, g2_ref[...]).astype(BF16)
    d_ff = wg_ref.shape[1]
    y = h
    for c in range(0, d_ff, FFN_CHUNK):
        g = _dot(hn, wg_ref[:, c:c + FFN_CHUNK])
        u = _dot(hn, wu_ref[:, c:c + FFN_CHUNK])
        act = (g * jax.nn.sigmoid(g) * u).astype(BF16)
        y = y + _dot(act, wdown_ref[c:c + FFN_CHUNK, :])
    if final_norm:
        y = _rmsnorm(y, gf_ref[...])
    y_ref[...] = y


def _post_call(x, att, g0, mpart, weights, rows, final_norm):
    nb, t_len, d = x.shape
    grid = (nb, t_len // rows)
    row_spec = lambda c: pl.BlockSpec((None, rows, c), lambda b, i: (b, i, 0))
    full = lambda a: pl.BlockSpec(a.shape, lambda b, i: (0,) * a.ndim,
                                  pipeline_mode=pl.Buffered(1))
    return pl.pallas_call(
        functools.partial(_post_kernel, final_norm=final_norm),
        grid=grid,
        in_specs=[row_spec(d), row_spec(ATT_W), row_spec(d), row_spec(d)]
                 + [full(w) for w in weights],
        out_specs=row_spec(d),
        out_shape=jax.ShapeDtypeStruct((nb, t_len, d), F32),
        compiler_params=pltpu.CompilerParams(
            dimension_semantics=("arbitrary", "arbitrary"), vmem_limit_bytes=VMEM_LIMIT),
        name="post",
    )(x, att, g0, mpart, *weights)


def _block_diag(w):
    n, d, e = w.shape
    eye = jnp.eye(n, dtype=w.dtype)
    return (w[:, :, None, :] * eye[:, None, :, None]).reshape(n * d, n * e)


def _row2(v):
    return v.reshape(1, -1).astype(F32)


def _history_rows(buf, t_len):
    nb, w1, c = buf.shape
    padded = jnp.concatenate([jnp.zeros((nb, t_len - w1, c), buf.dtype), buf], axis=1)
    return jnp.roll(padded, -1, axis=0).reshape(1, nb * t_len, c)


def _f_placement():
    place = np.zeros((F_PARTS * LANES, N_PAIRS * LANES), np.float32)
    for k in range(F_PARTS):
        for h in range(N_HEADS):
            place[k * LANES + h, (h // 2) * LANES + F_PARTS * (h % 2) + k] = 1.0
    return jnp.asarray(place, BF16)


def _paged_queries(q, db, t_dec):
    q4 = jnp.transpose(q.reshape(db, t_dec, N_HEADS, HEAD_DIM), (0, 2, 1, 3))
    eye = jnp.eye(N_HEADS, dtype=q.dtype)
    q_bd = (q4[:, :, :, None, :] * eye[None, :, None, :, None]).reshape(
        db, N_HEADS * t_dec, ATT_W)
    sel = -jnp.repeat(eye, t_dec, axis=0)
    aug = jnp.concatenate(
        [sel, sel, jnp.zeros((N_HEADS * t_dec, PAGED_K - ATT_W - 2 * N_HEADS), q.dtype)], axis=1)
    return jnp.concatenate([q_bd, jnp.broadcast_to(aug, (db,) + aug.shape)], axis=2)


def kernel(x_prompt, x_sample, cache_k, cache_v, cache_logf, state_lru_h, state_lru_conv,
           state_sc_conv, page_table, norm1_g, w_in, b_f, lru_conv_w, lru_conv_b, lru_w_a,
           lru_b_a, lru_w_x, lru_b_x, lru_lambda, sc_conv_w, w_gate, b_gate, w_br_att,
           w_br_lru, w_br_sc, w_out, norm2_g, w_ffn_in, w_ffn_out, final_norm_g):
    depth = w_in.shape[0]
    bp, s_len, d = x_prompt.shape
    db, t_dec, _ = x_sample.shape
    n_tok_s = db * t_dec
    d_ff = w_ffn_out.shape[1]
    rows_p = min(ROW_TILE, s_len)
    n_pool = cache_k.shape[1]

    cache_kt = jnp.transpose(cache_k, (0, 1, 3, 4, 2)).reshape(depth, n_pool, ATT_W, PAGE_SIZE)
    cache_vt = jnp.transpose(cache_v, (0, 1, 3, 4, 2)).reshape(depth, n_pool, ATT_W, PAGE_SIZE)
    cache_lft = jnp.transpose(cache_logf, (0, 1, 3, 2))
    place = _f_placement()

    yp = x_prompt
    ys = x_sample.reshape(1, n_tok_s, d)
    outs_p = [[] for _ in range(3)]
    outs_s = [[] for _ in range(6)]
    kv_all = None
    q_off = 3 * ATT_W
    r_off = q_off + N_HEADS
    for l in range(depth):
        wi = w_in[l]
        wqkv = wi[:, :q_off].astype(BF16)
        wfl = jnp.pad(wi[:, q_off:r_off], ((0, 0), (0, LANES - N_HEADS))).astype(BF16)
        bfl = jnp.pad(b_f[l].reshape(1, -1), ((0, 0), (0, LANES - N_HEADS))).astype(F32)
        g1 = _row2(norm1_g[l])
        mix_w = [g1, wi[:, r_off:].astype(BF16), lru_conv_w[l], _row2(lru_conv_b[l]),
                 _block_diag(lru_w_a[l]).astype(BF16), _row2(lru_b_a[l]),
                 _block_diag(lru_w_x[l]).astype(BF16), _row2(lru_b_x[l]), _row2(lru_lambda[l]),
                 sc_conv_w[l], w_gate[l].astype(BF16), _row2(b_gate[l]),
                 w_br_lru[l].astype(BF16), w_br_sc[l].astype(BF16)]
        post_w = [w_br_att[l].astype(BF16), w_out[l].astype(BF16), _row2(norm2_g[l]),
                  w_ffn_in[l][:, :d_ff].astype(BF16), w_ffn_in[l][:, d_ff:].astype(BF16),
                  w_ffn_out[l].astype(BF16), _row2(final_norm_g)]
        last = l == depth - 1

        q, k2, vt, *kv_all = _qkv_prompt_call(yp, g1, wqkv, wfl, bfl, place, rows_p, l, depth,
                                              kv_all)
        g0, mpart, xr_t, u_t, h_t = _mix_call(yp, mix_w, rows_p, rows_p)
        att = _flash_call(q, k2, vt)
        yp = _post_call(yp, att, g0, mpart, post_w, rows_p, last)
        outs_p[0].append(h_t[:, SUBLANES - 1])
        outs_p[1].append(xr_t[:, SUBLANES - (LRU_CONV - 1):])
        outs_p[2].append(u_t[:, SUBLANES - (SC_CONV - 1):])

        q, k, v, kb, vb, lf, fc = _qkv_sample_call(ys[0], g1, wqkv, wfl, bfl, t_dec)
        state = (_history_rows(state_lru_conv[l], t_dec), _history_rows(state_sc_conv[l], t_dec),
                 jnp.repeat(state_lru_h[l], t_dec, axis=0).reshape(1, n_tok_s, LRU_W))
        g0, mpart, xr_t, u_t, h_t = _mix_call(ys, mix_w, n_tok_s, t_dec, state)
        as_page = lambda a: jnp.pad(jnp.transpose(a.reshape(db, t_dec, -1), (0, 2, 1)),
                                    ((0, 0), (0, 0), (0, PAGE_SIZE - t_dec)))
        att = _paged_call(l, page_table, _paged_queries(q, db, t_dec), as_page(kb), as_page(vb),
                          as_page(fc), cache_kt, cache_vt, cache_lft)
        att = att.reshape(1, n_tok_s, ATT_W).astype(BF16)
        ys = _post_call(ys, att, g0, mpart, post_w, n_tok_s, last)
        outs_s[0].append(k.reshape(db, t_dec, N_HEADS, HEAD_DIM))
        outs_s[1].append(v.reshape(db, t_dec, N_HEADS, HEAD_DIM))
        outs_s[2].append(lf.reshape(db, t_dec, N_HEADS))
        outs_s[3].append(h_t.reshape(db, t_dec, LRU_W)[:, t_dec - 1])
        outs_s[4].append(xr_t.reshape(db, t_dec, LRU_W)[:, t_dec - (LRU_CONV - 1):])
        outs_s[5].append(u_t.reshape(db, t_dec, SC_W)[:, t_dec - (SC_CONV - 1):])

    kv_out = lambda o: jnp.transpose(
        o.reshape(depth, bp, N_HEADS, HEAD_DIM, s_len), (0, 1, 4, 2, 3))
    return (yp, ys.reshape(db, t_dec, d),
            kv_out(kv_all[0]), kv_out(kv_all[1]), jnp.transpose(kv_all[2], (0, 1, 3, 2)),
            *[jnp.stack(o) for o in outs_p], *[jnp.stack(o) for o in outs_s])
```
